```python
import jax, jax.numpy as jnp
from jax import lax
import numpy as np

D_MODEL = 1024
BATCH = 8
SEQ = 2048
DEPTH = 1
DEC_BATCH = 128
DEC_SEQ = 8
PAST_LEN = 16384
PAGE_SIZE = 128

POOL_WIDTH = D_MODEL // 2
CONV_WIDTH = D_MODEL - POOL_WIDTH
POOL_WINDOWS = (2, 4, 8, 16)
POOL_GROUPS = len(POOL_WINDOWS)
POOL_GROUP_DIM = POOL_WIDTH // POOL_GROUPS
POOL_HIST = max(POOL_WINDOWS) - 1
CONV_K = 3
FFN_K = 3
D_FF = ((8 * D_MODEL // 3 + 127) // 128) * 128
IN_WIDTH = POOL_WIDTH + 3 * CONV_WIDTH
EPS = 1e-6

kernel_name = "hybrid_pool_shortconv_convffn_step"


def rmsnorm(x, g):
    xf = x.astype(jnp.float32)
    y = xf * lax.rsqrt(jnp.mean(xf * xf, axis=-1, keepdims=True) + EPS)
    return (y * g.astype(jnp.float32)).astype(x.dtype)


def causal_dwconv(v_ext, w):
    k = w.shape[0]
    t = v_ext.shape[1] - (k - 1)
    out = w[0] * v_ext[:, 0:t]
    for j in range(1, k):
        out = out + w[j] * v_ext[:, j:j + t]
    return out


def pool_mix(u_ext, pos, w_grp, scale):
    n, l, c = u_ext.shape
    t = l - POOL_HIST
    uf = u_ext.astype(jnp.float32)
    cs = jnp.concatenate([jnp.zeros((n, 1, c), jnp.float32), jnp.cumsum(uf, axis=1)], axis=1)
    u_cur = uf[:, POOL_HIST:]
    posf = pos.astype(jnp.float32)
    diffs = []
    for g, win in enumerate(POOL_WINDOWS):
        sl = slice(g * POOL_GROUP_DIM, (g + 1) * POOL_GROUP_DIM)
        s = cs[:, POOL_HIST + 1:POOL_HIST + 1 + t, sl] - cs[:, POOL_HIST + 1 - win:POOL_HIST + 1 - win + t, sl]
        cnt = jnp.minimum(posf + 1.0, float(win))[None, :, None]
        diffs.append(s / cnt - u_cur[:, :, sl])
    d = jnp.stack(diffs, axis=2)
    y = jnp.einsum('ntgc,gcd->ntgd', d, w_grp.astype(jnp.float32)).reshape(n, t, c)
    return (y * scale.astype(jnp.float32)).astype(u_ext.dtype)


def layer(x, pos, st_pool, st_conv, st_ffn, g1, w_in, w_pool_grp, pool_scale, conv_w, w_out, g2, w_up, ffn_conv_w, w_down):
    h = rmsnorm(x, g1)
    z = jnp.einsum('ntd,de->nte', h, w_in)
    u = z[..., :POOL_WIDTH]
    gb = z[..., POOL_WIDTH:POOL_WIDTH + CONV_WIDTH]
    gc = z[..., POOL_WIDTH + CONV_WIDTH:POOL_WIDTH + 2 * CONV_WIDTH]
    hv = z[..., POOL_WIDTH + 2 * CONV_WIDTH:]
    u_ext = jnp.concatenate([st_pool.astype(u.dtype), u], axis=1)
    y_pool = pool_mix(u_ext, pos, w_pool_grp, pool_scale)
    new_pool = u_ext[:, -POOL_HIST:]
    vc_ext = jnp.concatenate([st_conv.astype(u.dtype), gc * hv], axis=1)
    y_conv = gb * causal_dwconv(vc_ext, conv_w)
    new_conv = vc_ext[:, -(CONV_K - 1):]
    x = x + jnp.einsum('nte,ed->ntd', jnp.concatenate([y_pool, y_conv], axis=-1), w_out)
    h2 = rmsnorm(x, g2)
    up = jnp.einsum('ntd,df->ntf', h2, w_up)
    up_ext = jnp.concatenate([st_ffn.astype(up.dtype), up], axis=1)
    upc = causal_dwconv(up_ext, ffn_conv_w)
    act = jax.nn.silu(upc[..., :D_FF]) * upc[..., D_FF:]
    x = x + jnp.einsum('ntf,fd->ntd', act, w_down)
    new_ffn = up_ext[:, -(FFN_K - 1):]
    return x, new_pool, new_conv, new_ffn


def setup_inputs(seed: int = 0) -> dict:
    key = jax.random.key(seed)
    ks = jax.random.split(key, 20)
    f32 = jnp.float32
    nrm = lambda k, s, sc: jax.random.normal(k, s, f32) * sc
    return {
        "x_prompt": nrm(ks[0], (BATCH, SEQ, D_MODEL), 1.0),
        "x_sample": nrm(ks[1], (DEC_BATCH, DEC_SEQ, D_MODEL), 1.0),
        "state_pool": nrm(ks[2], (DEPTH, DEC_BATCH, POOL_HIST, POOL_WIDTH), 1.0),
        "state_conv": nrm(ks[3], (DEPTH, DEC_BATCH, CONV_K - 1, CONV_WIDTH), 1.0),
        "state_ffn": nrm(ks[4], (DEPTH, DEC_BATCH, FFN_K - 1, 2 * D_FF), 1.0),
        "norm1_g": 1.0 + nrm(ks[5], (DEPTH, D_MODEL), 0.02),
        "w_in": nrm(ks[6], (DEPTH, D_MODEL, IN_WIDTH), D_MODEL ** -0.5),
        "w_pool_grp": nrm(ks[7], (DEPTH, POOL_GROUPS, POOL_GROUP_DIM, POOL_GROUP_DIM), POOL_GROUP_DIM ** -0.5),
        "pool_scale": 1.0 + nrm(ks[8], (DEPTH, POOL_WIDTH), 0.02),
        "conv_w": nrm(ks[9], (DEPTH, CONV_K, CONV_WIDTH), CONV_K ** -0.5),
        "w_out": nrm(ks[10], (DEPTH, D_MODEL, D_MODEL), D_MODEL ** -0.5),
        "norm2_g": 1.0 + nrm(ks[11], (DEPTH, D_MODEL), 0.02),
        "w_up": nrm(ks[12], (DEPTH, D_MODEL, 2 * D_FF), D_MODEL ** -0.5),
        "ffn_conv_w": nrm(ks[13], (DEPTH, FFN_K, 2 * D_FF), FFN_K ** -0.5),
        "w_down": nrm(ks[14], (DEPTH, D_FF, D_MODEL), D_FF ** -0.5),
        "final_g": 1.0 + nrm(ks[15], (D_MODEL,), 0.02),
    }


def reference(x_prompt, x_sample, state_pool, state_conv, state_ffn, norm1_g, w_in, w_pool_grp, pool_scale, conv_w, w_out, norm2_g, w_up, ffn_conv_w, w_down, final_g):
    bp = x_prompt.shape[0]
    pos_p = jnp.arange(x_prompt.shape[1], dtype=jnp.int32)
    pos_s = PAST_LEN + jnp.arange(x_sample.shape[1], dtype=jnp.int32)
    xp, xs = x_prompt, x_sample
    pp, cp, fp_, ps, cs_, fs = [], [], [], [], [], []
    for l in range(DEPTH):
        wl = (norm1_g[l], w_in[l], w_pool_grp[l], pool_scale[l], conv_w[l], w_out[l], norm2_g[l], w_up[l], ffn_conv_w[l], w_down[l])
        z_pool = jnp.zeros((bp, POOL_HIST, POOL_WIDTH), xp.dtype)
        z_conv = jnp.zeros((bp, CONV_K - 1, CONV_WIDTH), xp.dtype)
        z_ffn = jnp.zeros((bp, FFN_K - 1, 2 * D_FF), xp.dtype)
        xp, a, b, c = layer(xp, pos_p, z_pool, z_conv, z_ffn, *wl)
        pp.append(a); cp.append(b); fp_.append(c)
        xs, a, b, c = layer(xs, pos_s, state_pool[l], state_conv[l], state_ffn[l], *wl)
        ps.append(a); cs_.append(b); fs.append(c)
    y_prompt = rmsnorm(xp, final_g)
    y_sample = rmsnorm(xs, final_g)
    return (y_prompt, y_sample, jnp.stack(pp), jnp.stack(cp), jnp.stack(fp_), jnp.stack(ps), jnp.stack(cs_), jnp.stack(fs))
```

```python
import functools

import jax
import jax.numpy as jnp
from jax import lax
from jax.experimental import pallas as pl
from jax.experimental.pallas import tpu as pltpu

D_MODEL = 1024
POOL_WIDTH = 512
CONV_WIDTH = 512
POOL_WINDOWS = (2, 4, 8, 16)
GROUP_DIM = 128
POOL_HIST = 15
CONV_K = 3
D_FF = 2816
IN_WIDTH = POOL_WIDTH + 3 * CONV_WIDTH
EPS = 1e-6

SUBLANES = 8
TILE = 512
FF_CHUNK = 256
N_CHUNKS = D_FF // FF_CHUNK
POOL_PAD = 16
DEC_SEQ = 8
SEQ_PER_TILE = TILE // DEC_SEQ
VMEM_LIMIT_BYTES = 56 * 1024 * 1024

F32 = jnp.float32
BF16 = jnp.bfloat16


def _rmsnorm(x, g):
    ms = jnp.sum(x * x, axis=-1, keepdims=True) * (1.0 / D_MODEL)
    return (x * lax.rsqrt(ms + EPS)) * g


def _silu(x):
    return x * (1.0 / (1.0 + jnp.exp(-x)))


def _ff_cols(c):
    g0 = c * FF_CHUNK
    v0 = D_FF + c * FF_CHUNK
    return slice(g0, g0 + FF_CHUNK), slice(v0, v0 + FF_CHUNK)


def _layer_tail(x_ref, h_ref, mix_ref, x1_ref, y_ref, wout_ref, g2_ref, wup_ref, fcw_ref,
                wdown_ref, gf_ref, ffn_conv):
    x1 = x_ref[...] + jnp.dot(mix_ref[...], wout_ref[...], preferred_element_type=F32)
    x1_ref[...] = x1
    h_ref[...] = _rmsnorm(x1, g2_ref[...]).astype(BF16)
    for c in range(N_CHUNKS):
        gcols, vcols = _ff_cols(c)
        w_c = jnp.concatenate([wup_ref[:, gcols], wup_ref[:, vcols]], axis=1)
        up = jnp.dot(h_ref[...], w_c, preferred_element_type=F32)
        cw = jnp.concatenate([fcw_ref[:, gcols], fcw_ref[:, vcols]], axis=1)
        upc = ffn_conv(c, up, cw)
        act = _silu(upc[:, :FF_CHUNK]) * upc[:, FF_CHUNK:]
        x1_ref[...] += jnp.dot(act.astype(BF16), wdown_ref[gcols, :], preferred_element_type=F32)
    y_ref[...] = _rmsnorm(x1_ref[...], gf_ref[...])


def _taps(cw, s2, s1, s0):
    return cw[0:1, :] * s2 + cw[1:2, :] * s1 + cw[2:3, :] * s0


def _prompt_kernel(x_ref, g1_ref, win_ref, wgrp_ref, scale_ref, cw_ref, wout_ref, g2_ref, wup_ref,
                   fcw_ref, wdown_ref, gf_ref,
                   y_ref, po_ref, co_ref, fo_ref,
                   hu_ref, hc_ref, hf_ref, h_ref, z_ref, mix_ref, x1_ref):
    j = pl.program_id(1)

    @pl.when(j == 0)
    def _():
        hu_ref[...] = jnp.zeros_like(hu_ref)
        hc_ref[...] = jnp.zeros_like(hc_ref)
        hf_ref[...] = jnp.zeros_like(hf_ref)

    def shifted(hist, v, k):
        ext = jnp.concatenate([hist, v], axis=0)
        return pltpu.roll(ext, k, axis=0)[hist.shape[0]:]

    h_ref[...] = _rmsnorm(x_ref[...], g1_ref[...]).astype(BF16)
    z_ref[...] = jnp.dot(h_ref[...], win_ref[...], preferred_element_type=F32)

    pos = lax.broadcasted_iota(jnp.int32, (TILE, GROUP_DIM), 0) + j * TILE
    for g, win in enumerate(POOL_WINDOWS):
        cols = slice(g * GROUP_DIM, (g + 1) * GROUP_DIM)
        u = z_ref[:, cols]
        s = jnp.concatenate([hu_ref[:, cols], u], axis=0)
        step = 1
        while step < win:
            s = s + pltpu.roll(s, step, axis=0)
            step *= 2
        s = s[POOL_PAD:]
        cnt = jnp.minimum(pos + 1, win).astype(F32)
        d = s / cnt - u
        y = jnp.dot(d.astype(BF16), wgrp_ref[g], preferred_element_type=F32) * scale_ref[:, cols]
        mix_ref[:, cols] = y.astype(BF16)
    hu_ref[...] = z_ref[TILE - POOL_PAD:, 0:POOL_WIDTH]
    po_ref[...] = hu_ref[...]

    gb = z_ref[:, POOL_WIDTH:POOL_WIDTH + CONV_WIDTH]
    v = z_ref[:, POOL_WIDTH + CONV_WIDTH:POOL_WIDTH + 2 * CONV_WIDTH] * z_ref[:, POOL_WIDTH + 2 * CONV_WIDTH:]
    hc = hc_ref[...]
    yc = gb * _taps(cw_ref[...], shifted(hc, v, 2), shifted(hc, v, 1), v)
    mix_ref[:, POOL_WIDTH:] = yc.astype(BF16)
    hc_ref[...] = v[TILE - SUBLANES:]
    co_ref[...] = hc_ref[...]

    def ffn_conv(c, up, cw):
        gcols, vcols = _ff_cols(c)
        hist = jnp.concatenate([hf_ref[:, gcols], hf_ref[:, vcols]], axis=1)
        out = _taps(cw, shifted(hist, up, 2), shifted(hist, up, 1), up)
        last = up[TILE - SUBLANES:]
        hf_ref[:, gcols] = last[:, :FF_CHUNK]
        hf_ref[:, vcols] = last[:, FF_CHUNK:]
        return out

    _layer_tail(x_ref, h_ref, mix_ref, x1_ref, y_ref, wout_ref, g2_ref, wup_ref, fcw_ref, wdown_ref,
                gf_ref, ffn_conv)
    fo_ref[...] = hf_ref[...]


def _sample_kernel(x_ref, sp_ref, sc_ref, sf_ref, g1_ref, win_ref, wgrp_ref, scale_ref, cw_ref,
                   wout_ref, g2_ref, wup_ref, fcw_ref, wdown_ref, gf_ref,
                   y_ref, po_ref, co_ref, fo_ref,
                   h_ref, z_ref, mix_ref, x1_ref):
    n = SEQ_PER_TILE

    def seq_view(a):
        return a.reshape(n, DEC_SEQ, a.shape[-1])

    def flat_view(a):
        return a.reshape(n * DEC_SEQ, a.shape[-1])

    def rot(a, k):
        return pltpu.roll(a, k % DEC_SEQ, axis=1) if k % DEC_SEQ else a

    def conv2(cur, hist_ref, cols_list, cw):
        width = cur.shape[-1]
        t = lax.broadcasted_iota(jnp.int32, (n, DEC_SEQ, width), 1)
        h0 = jnp.concatenate([hist_ref[:, 0:1, cols] for cols in cols_list], axis=-1)
        h1 = jnp.concatenate([hist_ref[:, 1:2, cols] for cols in cols_list], axis=-1)
        h0 = jnp.broadcast_to(h0, cur.shape)
        h1 = jnp.broadcast_to(h1, cur.shape)
        s1 = jnp.where(t >= 1, rot(cur, 1), h1)
        s2 = jnp.where(t >= 2, rot(cur, 2), jnp.where(t == 0, h0, h1))
        return _taps(cw, flat_view(s2), flat_view(s1), flat_view(cur))

    x_flat = x_ref[...].reshape(TILE, D_MODEL)
    h_ref[...] = _rmsnorm(x_flat, g1_ref[...]).astype(BF16)
    z_ref[...] = jnp.dot(h_ref[...], win_ref[...], preferred_element_type=F32)

    t = lax.broadcasted_iota(jnp.int32, (n, DEC_SEQ, GROUP_DIM), 1)
    for g, win in enumerate(POOL_WINDOWS):
        cols = slice(g * GROUP_DIM, (g + 1) * GROUP_DIM)
        cur = seq_view(z_ref[:, cols])
        ha = sp_ref[:, 0:DEC_SEQ, cols]
        hb = sp_ref[:, POOL_HIST - DEC_SEQ:POOL_HIST, cols]
        s = cur
        for k in range(1, win):
            if k < DEC_SEQ:
                sk = jnp.where(t >= k, rot(cur, k), rot(hb, k))
            elif k == DEC_SEQ:
                sk = hb
            else:
                sk = jnp.where(t >= k - DEC_SEQ, rot(hb, k), rot(ha, k + 1))
            s = s + sk
        d = s * (1.0 / win) - cur
        y = jnp.dot(flat_view(d).astype(BF16), wgrp_ref[g], preferred_element_type=F32) * scale_ref[:, cols]
        mix_ref[:, cols] = y.astype(BF16)
        nxt = rot(cur, DEC_SEQ - 1)
        po_ref[:, 0:DEC_SEQ, cols] = jnp.where(t < DEC_SEQ - 1, rot(hb, DEC_SEQ - 1), nxt)
        po_ref[:, DEC_SEQ:POOL_HIST, cols] = nxt[:, 0:POOL_HIST - DEC_SEQ, :]

    gb = z_ref[:, POOL_WIDTH:POOL_WIDTH + CONV_WIDTH]
    v = seq_view(z_ref[:, POOL_WIDTH + CONV_WIDTH:POOL_WIDTH + 2 * CONV_WIDTH] * z_ref[:, POOL_WIDTH + 2 * CONV_WIDTH:])
    yc = gb * conv2(v, sc_ref, [slice(0, CONV_WIDTH)], cw_ref[...])
    mix_ref[:, POOL_WIDTH:] = yc.astype(BF16)
    co_ref[...] = rot(v, 2)[:, 0:CONV_K - 1, :]

    def ffn_conv(c, up, cw):
        gcols, vcols = _ff_cols(c)
        up3 = seq_view(up)
        out = conv2(up3, sf_ref, [gcols, vcols], cw)
        last = rot(up3, 2)[:, 0:CONV_K - 1, :]
        fo_ref[:, :, gcols] = last[:, :, :FF_CHUNK]
        fo_ref[:, :, vcols] = last[:, :, FF_CHUNK:]
        return out

    _layer_tail(_Flat(x_ref), h_ref, mix_ref, x1_ref, _Flat(y_ref), wout_ref, g2_ref, wup_ref, fcw_ref,
                wdown_ref, gf_ref, ffn_conv)


class _Flat:
    def __init__(self, ref):
        self._ref = ref

    def __getitem__(self, idx):
        a = self._ref[...]
        return a.reshape(a.shape[0] * a.shape[1], a.shape[2])

    def __setitem__(self, idx, value):
        s = self._ref.shape
        self._ref[...] = value.reshape(s)


def _whole(a):
    nd = a.ndim
    return pl.BlockSpec(a.shape, lambda *_: (0,) * nd, pipeline_mode=pl.Buffered(1))


def kernel(x_prompt, x_sample, state_pool, state_conv, state_ffn, norm1_g, w_in, w_pool_grp, pool_scale,
           conv_w, w_out, norm2_g, w_up, ffn_conv_w, w_down, final_g):
    depth = w_in.shape[0]
    assert depth == 1
    batch, seq, _ = x_prompt.shape
    dec_batch, dec_seq, _ = x_sample.shape
    assert seq % TILE == 0 and dec_seq == DEC_SEQ and dec_batch % SEQ_PER_TILE == 0

    weights = (
        norm1_g[0].reshape(1, D_MODEL),
        w_in[0].astype(BF16),
        w_pool_grp[0].astype(BF16),
        pool_scale[0].reshape(1, POOL_WIDTH),
        conv_w[0],
        w_out[0].astype(BF16),
        norm2_g[0].reshape(1, D_MODEL),
        w_up[0].astype(BF16),
        ffn_conv_w[0],
        w_down[0].astype(BF16),
        final_g.reshape(1, D_MODEL),
    )
    w_specs = [_whole(w) for w in weights]
    big_scratch = [
        pltpu.VMEM((TILE, D_MODEL), BF16),
        pltpu.VMEM((TILE, IN_WIDTH), F32),
        pltpu.VMEM((TILE, D_MODEL), BF16),
        pltpu.VMEM((TILE, D_MODEL), F32),
    ]

    n_tiles = seq // TILE
    yp, pp, cp, fp = pl.pallas_call(
        _prompt_kernel,
        grid=(batch, n_tiles),
        in_specs=[pl.BlockSpec((None, TILE, D_MODEL), lambda b, j: (b, j, 0))] + w_specs,
        out_specs=[
            pl.BlockSpec((None, TILE, D_MODEL), lambda b, j: (b, j, 0)),
            pl.BlockSpec((None, POOL_PAD, POOL_WIDTH), lambda b, j: (b, 0, 0)),
            pl.BlockSpec((None, SUBLANES, CONV_WIDTH), lambda b, j: (b, 0, 0)),
            pl.BlockSpec((None, SUBLANES, 2 * D_FF), lambda b, j: (b, 0, 0)),
        ],
        out_shape=[
            jax.ShapeDtypeStruct((batch, seq, D_MODEL), F32),
            jax.ShapeDtypeStruct((batch, POOL_PAD, POOL_WIDTH), F32),
            jax.ShapeDtypeStruct((batch, SUBLANES, CONV_WIDTH), F32),
            jax.ShapeDtypeStruct((batch, SUBLANES, 2 * D_FF), F32),
        ],
        scratch_shapes=[
            pltpu.VMEM((POOL_PAD, POOL_WIDTH), F32),
            pltpu.VMEM((SUBLANES, CONV_WIDTH), F32),
            pltpu.VMEM((SUBLANES, 2 * D_FF), F32),
        ] + big_scratch,
        compiler_params=pltpu.CompilerParams(
            dimension_semantics=("arbitrary", "arbitrary"), vmem_limit_bytes=VMEM_LIMIT_BYTES),
        name="prompt_layer",
    )(x_prompt, *weights)

    sp, sc, sf = state_pool[0], state_conv[0], state_ffn[0]
    def seq_block(a, buffers):
        return pl.BlockSpec((SEQ_PER_TILE,) + a.shape[1:], lambda i: (i, 0, 0),
                            pipeline_mode=pl.Buffered(buffers))

    ys, ps, cs, fs = pl.pallas_call(
        _sample_kernel,
        grid=(dec_batch // SEQ_PER_TILE,),
        in_specs=[seq_block(x_sample, 2), seq_block(sp, 1), seq_block(sc, 1), seq_block(sf, 1)] + w_specs,
        out_specs=[seq_block(x_sample, 2), seq_block(sp, 1), seq_block(sc, 1), seq_block(sf, 1)],
        out_shape=[
            jax.ShapeDtypeStruct(x_sample.shape, F32),
            jax.ShapeDtypeStruct(sp.shape, F32),
            jax.ShapeDtypeStruct(sc.shape, F32),
            jax.ShapeDtypeStruct(sf.shape, F32),
        ],
        scratch_shapes=big_scratch,
        compiler_params=pltpu.CompilerParams(
            dimension_semantics=("arbitrary",), vmem_limit_bytes=VMEM_LIMIT_BYTES),
        name="sample_layer",
    )(x_sample, sp, sc, sf, *weights)

    return (
        yp,
        ys,
        pp[None, :, POOL_PAD - POOL_HIST:, :],
        cp[None, :, SUBLANES - (CONV_K - 1):, :],
        fp[None, :, SUBLANES - (CONV_K - 1):, :],
        ps[None],
        cs[None],
        fs[None],
    )
```

```python
import jax
import jax.numpy as jnp
from jax import lax
from jax.experimental import pallas as pl
from jax.experimental.pallas import tpu as pltpu

D_MODEL = 1024
POOL_WIDTH = 512
CONV_WIDTH = 512
POOL_WINDOWS = (2, 4, 8, 16)
GROUP_DIM = 128
POOL_HIST = 15
CONV_K = 3
D_FF = 2816
IN_WIDTH = POOL_WIDTH + 3 * CONV_WIDTH
EPS = 1e-6

LANES = 128
SUBLANES = 8
TILE = 512
FF_CHUNK = 256
N_CHUNKS = D_FF // FF_CHUNK
CHUNK_SLABS = 2 * FF_CHUNK // LANES
POOL_PAD = 16
DEC_SEQ = 8
SEQ_PER_TILE = TILE // DEC_SEQ
VMEM_LIMIT_BYTES = 56 * 1024 * 1024

F32 = jnp.float32
BF16 = jnp.bfloat16


def _rmsnorm(x, g):
    ms = jnp.sum(x * x, axis=-1, keepdims=True) * (1.0 / D_MODEL)
    return (x * lax.rsqrt(ms + EPS)) * g


def _silu(x):
    return x * (1.0 / (1.0 + jnp.exp(-x)))


def _lanes(i):
    return slice(i * LANES, (i + 1) * LANES)


def _ff_cols(c):
    g0 = c * FF_CHUNK
    v0 = D_FF + c * FF_CHUNK
    return slice(g0, g0 + FF_CHUNK), slice(v0, v0 + FF_CHUNK)


def _ff_slab_cols(c, s):
    gcols, vcols = _ff_cols(c)
    half = FF_CHUNK // LANES
    base = gcols.start if s < half else vcols.start
    off = (s % half) * LANES
    return slice(base + off, base + off + LANES)


def _taps(cw, s2, s1, s0):
    return cw[0:1, :] * s2 + cw[1:2, :] * s1 + cw[2:3, :] * s0


def _proj(h_ref, w_ref, cols):
    return jnp.dot(h_ref[...], w_ref[:, cols], preferred_element_type=F32)


def _conv_ffn(h_ref, act_ref, wup_ref, fcw_ref, put_up, conv_up):
    def up_proj(c):
        gcols, vcols = _ff_cols(c)
        w_c = jnp.concatenate([wup_ref[:, gcols], wup_ref[:, vcols]], axis=1)
        return jnp.dot(h_ref[...], w_c, preferred_element_type=F32)

    put_up(0, up_proj(0))
    for c in range(N_CHUNKS):
        if c + 1 < N_CHUNKS:
            put_up(c + 1, up_proj(c + 1))
        gcols, vcols = _ff_cols(c)
        cw = jnp.concatenate([fcw_ref[:, gcols], fcw_ref[:, vcols]], axis=1)
        upc = conv_up(c, cw)
        act = _silu(upc[:, :FF_CHUNK]) * upc[:, FF_CHUNK:]
        act_ref[:, gcols] = act.astype(BF16)


def _prompt_kernel(x_ref, g1_ref, win_ref, wgrp_ref, scale_ref, cw_ref, wout_ref, g2_ref, wup_ref,
                   fcw_ref, wdown_ref, gf_ref,
                   y_ref, po_ref, co_ref, fo_ref,
                   hu_ref, hc_ref, hf_ref, h_ref, mix_ref, act_ref, ubuf, cbuf, fbuf):
    j = pl.program_id(1)

    @pl.when(j == 0)
    def _():
        hu_ref[...] = jnp.zeros_like(hu_ref)
        hc_ref[...] = jnp.zeros_like(hc_ref)
        hf_ref[...] = jnp.zeros_like(hf_ref)

    h_ref[...] = _rmsnorm(x_ref[...], g1_ref[...]).astype(BF16)

    u = _proj(h_ref, win_ref, slice(0, POOL_WIDTH))
    for g in range(len(POOL_WINDOWS)):
        ubuf[g, 0:POOL_PAD, :] = hu_ref[:, _lanes(g)]
        ubuf[g, POOL_PAD:, :] = u[:, _lanes(g)]
        hu_ref[:, _lanes(g)] = u[TILE - POOL_PAD:, _lanes(g)]
    po_ref[...] = hu_ref[...]

    gc = _proj(h_ref, win_ref, slice(POOL_WIDTH + CONV_WIDTH, POOL_WIDTH + 2 * CONV_WIDTH))
    hv = _proj(h_ref, win_ref, slice(POOL_WIDTH + 2 * CONV_WIDTH, IN_WIDTH))
    v = gc * hv
    for s in range(CONV_WIDTH // LANES):
        cbuf[s, 0:SUBLANES, :] = hc_ref[:, _lanes(s)]
        cbuf[s, SUBLANES:, :] = v[:, _lanes(s)]
        hc_ref[:, _lanes(s)] = v[TILE - SUBLANES:, _lanes(s)]
    co_ref[...] = hc_ref[...]
    gb = _proj(h_ref, win_ref, slice(POOL_WIDTH, POOL_WIDTH + CONV_WIDTH))

    pos = lax.broadcasted_iota(jnp.int32, (TILE, GROUP_DIM), 0) + j * TILE
    for g, win in enumerate(POOL_WINDOWS):
        cur = ubuf[g, POOL_PAD:, :]
        s = cur
        for k in range(1, win):
            s = s + ubuf[g, POOL_PAD - k:POOL_PAD - k + TILE, :]
        cnt = jnp.minimum(pos + 1, win).astype(F32)
        d = s / cnt - cur
        y = jnp.dot(d.astype(BF16), wgrp_ref[g], preferred_element_type=F32) * scale_ref[:, _lanes(g)]
        mix_ref[:, _lanes(g)] = y.astype(BF16)

    for s in range(CONV_WIDTH // LANES):
        taps = [cbuf[s, SUBLANES - k:SUBLANES - k + TILE, :] for k in (2, 1, 0)]
        yc = gb[:, _lanes(s)] * _taps(cw_ref[:, _lanes(s)], *taps)
        mix_ref[:, POOL_WIDTH + s * LANES:POOL_WIDTH + (s + 1) * LANES] = yc.astype(BF16)

    x1 = x_ref[...] + jnp.dot(mix_ref[...], wout_ref[...], preferred_element_type=F32)
    h_ref[...] = _rmsnorm(x1, g2_ref[...]).astype(BF16)

    def put_up(c, up):
        buf = fbuf.at[c % 2]
        for s in range(CHUNK_SLABS):
            cols = _ff_slab_cols(c, s)
            buf[s, 0:SUBLANES, :] = hf_ref[:, cols]
            buf[s, SUBLANES:, :] = up[:, _lanes(s)]
            hf_ref[:, cols] = up[TILE - SUBLANES:, _lanes(s)]

    def conv_up(c, cw):
        buf = fbuf.at[c % 2]
        outs = []
        for s in range(CHUNK_SLABS):
            taps = [buf[s, SUBLANES - k:SUBLANES - k + TILE, :] for k in (2, 1, 0)]
            outs.append(_taps(cw[:, _lanes(s)], *taps))
        return jnp.concatenate(outs, axis=1)

    _conv_ffn(h_ref, act_ref, wup_ref, fcw_ref, put_up, conv_up)
    fo_ref[...] = hf_ref[...]
    x2 = x1 + jnp.dot(act_ref[...], wdown_ref[...], preferred_element_type=F32)
    y_ref[...] = _rmsnorm(x2, gf_ref[...])


def _sample_kernel(x_ref, sp_ref, sc_ref, sf_ref, g1_ref, win_ref, wgrp_ref, scale_ref, cw_ref,
                   wout_ref, g2_ref, wup_ref, fcw_ref, wdown_ref, gf_ref,
                   y_ref, po_ref, co_ref, fo_ref,
                   h_ref, mix_ref, act_ref):
    n = SEQ_PER_TILE

    def seq_view(a):
        return a.reshape(n, DEC_SEQ, a.shape[-1])

    def flat_view(a):
        return a.reshape(n * DEC_SEQ, a.shape[-1])

    def rot(a, k):
        return pltpu.roll(a, k % DEC_SEQ, axis=1) if k % DEC_SEQ else a

    def conv2(cur, hist_ref, cols_list, cw):
        width = cur.shape[-1]
        t = lax.broadcasted_iota(jnp.int32, (n, DEC_SEQ, width), 1)
        h0 = jnp.concatenate([hist_ref[:, 0:1, cols] for cols in cols_list], axis=-1)
        h1 = jnp.concatenate([hist_ref[:, 1:2, cols] for cols in cols_list], axis=-1)
        h0 = jnp.broadcast_to(h0, cur.shape)
        h1 = jnp.broadcast_to(h1, cur.shape)
        s1 = jnp.where(t >= 1, rot(cur, 1), h1)
        s2 = jnp.where(t >= 2, rot(cur, 2), jnp.where(t == 0, h0, h1))
        return _taps(cw, flat_view(s2), flat_view(s1), flat_view(cur))

    x = x_ref[...].reshape(TILE, D_MODEL)
    h_ref[...] = _rmsnorm(x, g1_ref[...]).astype(BF16)

    u = _proj(h_ref, win_ref, slice(0, POOL_WIDTH))
    t = lax.broadcasted_iota(jnp.int32, (n, DEC_SEQ, GROUP_DIM), 1)
    for g, win in enumerate(POOL_WINDOWS):
        cols = _lanes(g)
        cur = seq_view(u[:, cols])
        ha = sp_ref[:, 0:DEC_SEQ, cols]
        hb = sp_ref[:, POOL_HIST - DEC_SEQ:POOL_HIST, cols]
        s = cur
        for k in range(1, win):
            if k < DEC_SEQ:
                sk = jnp.where(t >= k, rot(cur, k), rot(hb, k))
            elif k == DEC_SEQ:
                sk = hb
            else:
                sk = jnp.where(t >= k - DEC_SEQ, rot(hb, k), rot(ha, k + 1))
            s = s + sk
        d = s * (1.0 / win) - cur
        y = jnp.dot(flat_view(d).astype(BF16), wgrp_ref[g], preferred_element_type=F32) * scale_ref[:, cols]
        mix_ref[:, cols] = y.astype(BF16)
        nxt = rot(cur, DEC_SEQ - 1)
        po_ref[:, 0:DEC_SEQ, cols] = jnp.where(t < DEC_SEQ - 1, rot(hb, DEC_SEQ - 1), nxt)
        po_ref[:, DEC_SEQ:POOL_HIST, cols] = nxt[:, 0:POOL_HIST - DEC_SEQ, :]

    gc = _proj(h_ref, win_ref, slice(POOL_WIDTH + CONV_WIDTH, POOL_WIDTH + 2 * CONV_WIDTH))
    hv = _proj(h_ref, win_ref, slice(POOL_WIDTH + 2 * CONV_WIDTH, IN_WIDTH))
    v = seq_view(gc * hv)
    gb = _proj(h_ref, win_ref, slice(POOL_WIDTH, POOL_WIDTH + CONV_WIDTH))
    yc = gb * conv2(v, sc_ref, [slice(0, CONV_WIDTH)], cw_ref[...])
    mix_ref[:, POOL_WIDTH:] = yc.astype(BF16)
    co_ref[...] = rot(v, 2)[:, 0:CONV_K - 1, :]

    x1 = x + jnp.dot(mix_ref[...], wout_ref[...], preferred_element_type=F32)
    h_ref[...] = _rmsnorm(x1, g2_ref[...]).astype(BF16)

    ups = {}

    def put_up(c, up):
        ups[c] = seq_view(up)

    def conv_up(c, cw):
        gcols, vcols = _ff_cols(c)
        up3 = ups.pop(c)
        last = rot(up3, 2)[:, 0:CONV_K - 1, :]
        fo_ref[:, :, gcols] = last[:, :, :FF_CHUNK]
        fo_ref[:, :, vcols] = last[:, :, FF_CHUNK:]
        return conv2(up3, sf_ref, [gcols, vcols], cw)

    _conv_ffn(h_ref, act_ref, wup_ref, fcw_ref, put_up, conv_up)
    x2 = x1 + jnp.dot(act_ref[...], wdown_ref[...], preferred_element_type=F32)
    y_ref[...] = _rmsnorm(x2, gf_ref[...]).reshape(n, DEC_SEQ, D_MODEL)


def _whole(a):
    nd = a.ndim
    return pl.BlockSpec(a.shape, lambda *_: (0,) * nd, pipeline_mode=pl.Buffered(1))


def kernel(x_prompt, x_sample, state_pool, state_conv, state_ffn, norm1_g, w_in, w_pool_grp, pool_scale,
           conv_w, w_out, norm2_g, w_up, ffn_conv_w, w_down, final_g):
    depth = w_in.shape[0]
    assert depth == 1
    batch, seq, _ = x_prompt.shape
    dec_batch, dec_seq, _ = x_sample.shape
    assert seq % TILE == 0 and dec_seq == DEC_SEQ and dec_batch % SEQ_PER_TILE == 0

    weights = (
        norm1_g[0].reshape(1, D_MODEL),
        w_in[0].astype(BF16),
        w_pool_grp[0].astype(BF16),
        pool_scale[0].reshape(1, POOL_WIDTH),
        conv_w[0],
        w_out[0].astype(BF16),
        norm2_g[0].reshape(1, D_MODEL),
        w_up[0].astype(BF16),
        ffn_conv_w[0],
        w_down[0].astype(BF16),
        final_g.reshape(1, D_MODEL),
    )
    w_specs = [_whole(w) for w in weights]
    tile_scratch = [
        pltpu.VMEM((TILE, D_MODEL), BF16),
        pltpu.VMEM((TILE, D_MODEL), BF16),
        pltpu.VMEM((TILE, D_FF), BF16),
    ]

    n_tiles = seq // TILE
    yp, pp, cp, fp = pl.pallas_call(
        _prompt_kernel,
        grid=(batch, n_tiles),
        in_specs=[pl.BlockSpec((None, TILE, D_MODEL), lambda b, j: (b, j, 0))] + w_specs,
        out_specs=[
            pl.BlockSpec((None, TILE, D_MODEL), lambda b, j: (b, j, 0)),
            pl.BlockSpec((None, POOL_PAD, POOL_WIDTH), lambda b, j: (b, 0, 0)),
            pl.BlockSpec((None, SUBLANES, CONV_WIDTH), lambda b, j: (b, 0, 0)),
            pl.BlockSpec((None, SUBLANES, 2 * D_FF), lambda b, j: (b, 0, 0)),
        ],
        out_shape=[
            jax.ShapeDtypeStruct((batch, seq, D_MODEL), F32),
            jax.ShapeDtypeStruct((batch, POOL_PAD, POOL_WIDTH), F32),
            jax.ShapeDtypeStruct((batch, SUBLANES, CONV_WIDTH), F32),
            jax.ShapeDtypeStruct((batch, SUBLANES, 2 * D_FF), F32),
        ],
        scratch_shapes=[
            pltpu.VMEM((POOL_PAD, POOL_WIDTH), F32),
            pltpu.VMEM((SUBLANES, CONV_WIDTH), F32),
            pltpu.VMEM((SUBLANES, 2 * D_FF), F32),
        ] + tile_scratch + [
            pltpu.VMEM((POOL_WIDTH // LANES, POOL_PAD + TILE, LANES), F32),
            pltpu.VMEM((CONV_WIDTH // LANES, SUBLANES + TILE, LANES), F32),
            pltpu.VMEM((2, CHUNK_SLABS, SUBLANES + TILE, LANES), F32),
        ],
        compiler_params=pltpu.CompilerParams(
            dimension_semantics=("arbitrary", "arbitrary"), vmem_limit_bytes=VMEM_LIMIT_BYTES),
        name="prompt_layer",
    )(x_prompt, *weights)

    sp, sc, sf = state_pool[0], state_conv[0], state_ffn[0]

    def seq_block(a, buffers):
        return pl.BlockSpec((SEQ_PER_TILE,) + a.shape[1:], lambda i: (i, 0, 0),
                            pipeline_mode=pl.Buffered(buffers))

    ys, ps, cs, fs = pl.pallas_call(
        _sample_kernel,
        grid=(dec_batch // SEQ_PER_TILE,),
        in_specs=[seq_block(x_sample, 2), seq_block(sp, 1), seq_block(sc, 1), seq_block(sf, 1)] + w_specs,
        out_specs=[seq_block(x_sample, 2), seq_block(sp, 1), seq_block(sc, 1), seq_block(sf, 1)],
        out_shape=[
            jax.ShapeDtypeStruct(x_sample.shape, F32),
            jax.ShapeDtypeStruct(sp.shape, F32),
            jax.ShapeDtypeStruct(sc.shape, F32),
            jax.ShapeDtypeStruct(sf.shape, F32),
        ],
        scratch_shapes=tile_scratch,
        compiler_params=pltpu.CompilerParams(
            dimension_semantics=("arbitrary",), vmem_limit_bytes=VMEM_LIMIT_BYTES),
        name="sample_layer",
    )(x_sample, sp, sc, sf, *weights)

    return (
        yp,
        ys,
        pp[None, :, POOL_PAD - POOL_HIST:, :],
        cp[None, :, SUBLANES - (CONV_K - 1):, :],
        fp[None, :, SUBLANES - (CONV_K - 1):, :],
        ps[None],
        cs[None],
        fs[None],
    )
```

```python
import jax
import jax.numpy as jnp
from jax import lax
from jax.experimental import pallas as pl
from jax.experimental.pallas import tpu as pltpu

D_MODEL = 1024
POOL_WIDTH = 512
CONV_WIDTH = 512
POOL_WINDOWS = (2, 4, 8, 16)
GROUP_DIM = 128
POOL_HIST = 15
CONV_K = 3
D_FF = 2816
IN_WIDTH = POOL_WIDTH + 3 * CONV_WIDTH
EPS = 1e-6

LANES = 128
SUBLANES = 8
TILE = 512
FF_CHUNK = 256
N_CHUNKS = D_FF // FF_CHUNK
CHUNK_SLABS = 2 * FF_CHUNK // LANES
FF_BUFFERS = 4
POOL_PAD = 16
DEC_SEQ = 8
SEQ_PER_TILE = TILE // DEC_SEQ
VMEM_LIMIT_BYTES = 56 * 1024 * 1024

F32 = jnp.float32
BF16 = jnp.bfloat16


def _rmsnorm(x, g):
    ms = jnp.sum(x * x, axis=-1, keepdims=True) * (1.0 / D_MODEL)
    return (x * lax.rsqrt(ms + EPS)) * g


def _silu(x):
    return x * (1.0 / (1.0 + jnp.exp(-x)))


def _lanes(i):
    return slice(i * LANES, (i + 1) * LANES)


def _ff_cols(c):
    g0 = c * FF_CHUNK
    v0 = D_FF + c * FF_CHUNK
    return slice(g0, g0 + FF_CHUNK), slice(v0, v0 + FF_CHUNK)


def _ff_slab_cols(c, s):
    gcols, vcols = _ff_cols(c)
    half = FF_CHUNK // LANES
    base = gcols.start if s < half else vcols.start
    off = (s % half) * LANES
    return slice(base + off, base + off + LANES)


def _taps(cw, s2, s1, s0):
    return cw[0:1, :] * s2 + cw[1:2, :] * s1 + cw[2:3, :] * s0


def _proj(h_ref, w_ref, cols):
    return jnp.dot(h_ref[...], w_ref[:, cols], preferred_element_type=F32)


def _conv_ffn(slot_ref, h_ref, act_refs, wup_ref, fcw_ref, wdown_ref, put_up, conv_up):
    def up_proj(c):
        gcols, vcols = _ff_cols(c)
        w_c = jnp.concatenate([wup_ref[:, gcols], wup_ref[:, vcols]], axis=1)
        return jnp.dot(h_ref[...], w_c, preferred_element_type=F32)

    put_up(0, up_proj(0))
    for c in range(N_CHUNKS):
        if c + 1 < N_CHUNKS:
            put_up(c + 1, up_proj(c + 1))
        gcols, vcols = _ff_cols(c)
        cw = jnp.concatenate([fcw_ref[:, gcols], fcw_ref[:, vcols]], axis=1)
        upc = conv_up(c, cw)
        act = _silu(upc[:, :FF_CHUNK]) * upc[:, FF_CHUNK:]
        act_refs[c][slot_ref[0]] = act.astype(BF16)

    down = None
    for c in range(N_CHUNKS):
        gcols, _ = _ff_cols(c)
        part = jnp.dot(act_refs[c][slot_ref[1]], wdown_ref[gcols, :], preferred_element_type=F32)
        down = part if down is None else down + part
    return down


def _prompt_kernel(slot_ref, x_ref, g1_ref, win_ref, wgrp_ref, scale_ref, cw_ref, wout_ref, g2_ref, wup_ref,
                   fcw_ref, wdown_ref, gf_ref,
                   y_ref, po_ref, co_ref, fo_ref,
                   hu_ref, hc_ref, hf_ref, h_ref, mix_ref, ubuf, cbuf, fbuf, *act_refs):
    j = pl.program_id(1)

    @pl.when(j == 0)
    def _():
        hu_ref[...] = jnp.zeros_like(hu_ref)
        hc_ref[...] = jnp.zeros_like(hc_ref)
        hf_ref[...] = jnp.zeros_like(hf_ref)

    h_ref[...] = _rmsnorm(x_ref[...], g1_ref[...]).astype(BF16)

    u = _proj(h_ref, win_ref, slice(0, POOL_WIDTH))
    for g in range(len(POOL_WINDOWS)):
        ubuf[g, 0:POOL_PAD, :] = hu_ref[:, _lanes(g)]
        ubuf[g, POOL_PAD:, :] = u[:, _lanes(g)]
        hu_ref[:, _lanes(g)] = u[TILE - POOL_PAD:, _lanes(g)]
    po_ref[...] = hu_ref[...]

    gc = _proj(h_ref, win_ref, slice(POOL_WIDTH + CONV_WIDTH, POOL_WIDTH + 2 * CONV_WIDTH))
    hv = _proj(h_ref, win_ref, slice(POOL_WIDTH + 2 * CONV_WIDTH, IN_WIDTH))
    v = gc * hv
    for s in range(CONV_WIDTH // LANES):
        cbuf[s, 0:SUBLANES, :] = hc_ref[:, _lanes(s)]
        cbuf[s, SUBLANES:, :] = v[:, _lanes(s)]
        hc_ref[:, _lanes(s)] = v[TILE - SUBLANES:, _lanes(s)]
    co_ref[...] = hc_ref[...]
    gb = _proj(h_ref, win_ref, slice(POOL_WIDTH, POOL_WIDTH + CONV_WIDTH))

    pos = lax.broadcasted_iota(jnp.int32, (TILE, GROUP_DIM), 0) + j * TILE
    for g, win in enumerate(POOL_WINDOWS):
        cur = ubuf[g, POOL_PAD:, :]
        s = cur
        for k in range(1, win):
            s = s + ubuf[g, POOL_PAD - k:POOL_PAD - k + TILE, :]
        cnt = jnp.minimum(pos + 1, win).astype(F32)
        d = s / cnt - cur
        y = jnp.dot(d.astype(BF16), wgrp_ref[g], preferred_element_type=F32) * scale_ref[:, _lanes(g)]
        mix_ref[:, _lanes(g)] = y.astype(BF16)

    for s in range(CONV_WIDTH // LANES):
        taps = [cbuf[s, SUBLANES - k:SUBLANES - k + TILE, :] for k in (2, 1, 0)]
        yc = gb[:, _lanes(s)] * _taps(cw_ref[:, _lanes(s)], *taps)
        mix_ref[:, POOL_WIDTH + s * LANES:POOL_WIDTH + (s + 1) * LANES] = yc.astype(BF16)

    x1 = x_ref[...] + jnp.dot(mix_ref[...], wout_ref[...], preferred_element_type=F32)
    h_ref[...] = _rmsnorm(x1, g2_ref[...]).astype(BF16)

    def put_up(c, up):
        buf = fbuf.at[c % FF_BUFFERS]
        for s in range(CHUNK_SLABS):
            cols = _ff_slab_cols(c, s)
            buf[s, 0:SUBLANES, :] = hf_ref[:, cols]
            buf[s, SUBLANES:, :] = up[:, _lanes(s)]
            hf_ref[:, cols] = up[TILE - SUBLANES:, _lanes(s)]

    def conv_up(c, cw):
        buf = fbuf.at[c % FF_BUFFERS]
        outs = []
        for s in range(CHUNK_SLABS):
            taps = [buf[s, SUBLANES - k:SUBLANES - k + TILE, :] for k in (2, 1, 0)]
            outs.append(_taps(cw[:, _lanes(s)], *taps))
        return jnp.concatenate(outs, axis=1)

    x2 = x1 + _conv_ffn(slot_ref, h_ref, act_refs, wup_ref, fcw_ref, wdown_ref, put_up, conv_up)
    fo_ref[...] = hf_ref[...]
    y_ref[...] = _rmsnorm(x2, gf_ref[...])


def _sample_kernel(slot_ref, x_ref, sp_ref, sc_ref, sf_ref, g1_ref, win_ref, wgrp_ref, scale_ref, cw_ref,
                   wout_ref, g2_ref, wup_ref, fcw_ref, wdown_ref, gf_ref,
                   y_ref, po_ref, co_ref, fo_ref,
                   h_ref, mix_ref, *act_refs):
    n = SEQ_PER_TILE

    def seq_view(a):
        return a.reshape(n, DEC_SEQ, a.shape[-1])

    def flat_view(a):
        return a.reshape(n * DEC_SEQ, a.shape[-1])

    def rot(a, k):
        return pltpu.roll(a, k % DEC_SEQ, axis=1) if k % DEC_SEQ else a

    def conv2(cur, hist_ref, cols_list, cw):
        width = cur.shape[-1]
        t = lax.broadcasted_iota(jnp.int32, (n, DEC_SEQ, width), 1)
        h0 = jnp.concatenate([hist_ref[:, 0:1, cols] for cols in cols_list], axis=-1)
        h1 = jnp.concatenate([hist_ref[:, 1:2, cols] for cols in cols_list], axis=-1)
        h0 = jnp.broadcast_to(h0, cur.shape)
        h1 = jnp.broadcast_to(h1, cur.shape)
        s1 = jnp.where(t >= 1, rot(cur, 1), h1)
        s2 = jnp.where(t >= 2, rot(cur, 2), jnp.where(t == 0, h0, h1))
        return _taps(cw, flat_view(s2), flat_view(s1), flat_view(cur))

    x = x_ref[...].reshape(TILE, D_MODEL)
    h_ref[...] = _rmsnorm(x, g1_ref[...]).astype(BF16)

    u = _proj(h_ref, win_ref, slice(0, POOL_WIDTH))
    t = lax.broadcasted_iota(jnp.int32, (n, DEC_SEQ, GROUP_DIM), 1)
    for g, win in enumerate(POOL_WINDOWS):
        cols = _lanes(g)
        cur = seq_view(u[:, cols])
        ha = sp_ref[:, 0:DEC_SEQ, cols]
        hb = sp_ref[:, POOL_HIST - DEC_SEQ:POOL_HIST, cols]
        s = cur
        for k in range(1, win):
            if k < DEC_SEQ:
                sk = jnp.where(t >= k, rot(cur, k), rot(hb, k))
            elif k == DEC_SEQ:
                sk = hb
            else:
                sk = jnp.where(t >= k - DEC_SEQ, rot(hb, k), rot(ha, k + 1))
            s = s + sk
        d = s * (1.0 / win) - cur
        y = jnp.dot(flat_view(d).astype(BF16), wgrp_ref[g], preferred_element_type=F32) * scale_ref[:, cols]
        mix_ref[:, cols] = y.astype(BF16)
        nxt = rot(cur, DEC_SEQ - 1)
        po_ref[:, 0:DEC_SEQ, cols] = jnp.where(t < DEC_SEQ - 1, rot(hb, DEC_SEQ - 1), nxt)
        po_ref[:, DEC_SEQ:POOL_HIST, cols] = nxt[:, 0:POOL_HIST - DEC_SEQ, :]

    gc = _proj(h_ref, win_ref, slice(POOL_WIDTH + CONV_WIDTH, POOL_WIDTH + 2 * CONV_WIDTH))
    hv = _proj(h_ref, win_ref, slice(POOL_WIDTH + 2 * CONV_WIDTH, IN_WIDTH))
    v = seq_view(gc * hv)
    gb = _proj(h_ref, win_ref, slice(POOL_WIDTH, POOL_WIDTH + CONV_WIDTH))
    yc = gb * conv2(v, sc_ref, [slice(0, CONV_WIDTH)], cw_ref[...])
    mix_ref[:, POOL_WIDTH:] = yc.astype(BF16)
    co_ref[...] = rot(v, 2)[:, 0:CONV_K - 1, :]

    x1 = x + jnp.dot(mix_ref[...], wout_ref[...], preferred_element_type=F32)
    h_ref[...] = _rmsnorm(x1, g2_ref[...]).astype(BF16)

    ups = {}

    def put_up(c, up):
        ups[c] = seq_view(up)

    def conv_up(c, cw):
        gcols, vcols = _ff_cols(c)
        up3 = ups.pop(c)
        last = rot(up3, 2)[:, 0:CONV_K - 1, :]
        fo_ref[:, :, gcols] = last[:, :, :FF_CHUNK]
        fo_ref[:, :, vcols] = last[:, :, FF_CHUNK:]
        return conv2(up3, sf_ref, [gcols, vcols], cw)

    x2 = x1 + _conv_ffn(slot_ref, h_ref, act_refs, wup_ref, fcw_ref, wdown_ref, put_up, conv_up)
    y_ref[...] = _rmsnorm(x2, gf_ref[...]).reshape(n, DEC_SEQ, D_MODEL)


def _whole(a):
    nd = a.ndim
    return pl.BlockSpec(a.shape, lambda *_: (0,) * nd, pipeline_mode=pl.Buffered(1))


def kernel(x_prompt, x_sample, state_pool, state_conv, state_ffn, norm1_g, w_in, w_pool_grp, pool_scale,
           conv_w, w_out, norm2_g, w_up, ffn_conv_w, w_down, final_g):
    depth = w_in.shape[0]
    assert depth == 1
    batch, seq, _ = x_prompt.shape
    dec_batch, dec_seq, _ = x_sample.shape
    assert seq % TILE == 0 and dec_seq == DEC_SEQ and dec_batch % SEQ_PER_TILE == 0

    weights = (
        norm1_g[0].reshape(1, D_MODEL),
        w_in[0].astype(BF16),
        w_pool_grp[0].astype(BF16),
        pool_scale[0].reshape(1, POOL_WIDTH),
        conv_w[0],
        w_out[0].astype(BF16),
        norm2_g[0].reshape(1, D_MODEL),
        w_up[0].astype(BF16),
        ffn_conv_w[0],
        w_down[0].astype(BF16),
        final_g.reshape(1, D_MODEL),
    )
    w_specs = [_whole(w) for w in weights]
    slots = jnp.zeros((2,), jnp.int32)
    slot_spec = pl.BlockSpec(memory_space=pltpu.SMEM)
    tile_scratch = [
        pltpu.VMEM((TILE, D_MODEL), BF16),
        pltpu.VMEM((TILE, D_MODEL), BF16),
    ]
    act_scratch = [pltpu.VMEM((1, TILE, FF_CHUNK), BF16)] * N_CHUNKS

    n_tiles = seq // TILE
    yp, pp, cp, fp = pl.pallas_call(
        _prompt_kernel,
        grid=(batch, n_tiles),
        in_specs=[slot_spec, pl.BlockSpec((None, TILE, D_MODEL), lambda b, j: (b, j, 0))] + w_specs,
        out_specs=[
            pl.BlockSpec((None, TILE, D_MODEL), lambda b, j: (b, j, 0)),
            pl.BlockSpec((None, POOL_PAD, POOL_WIDTH), lambda b, j: (b, 0, 0)),
            pl.BlockSpec((None, SUBLANES, CONV_WIDTH), lambda b, j: (b, 0, 0)),
            pl.BlockSpec((None, SUBLANES, 2 * D_FF), lambda b, j: (b, 0, 0)),
        ],
        out_shape=[
            jax.ShapeDtypeStruct((batch, seq, D_MODEL), F32),
            jax.ShapeDtypeStruct((batch, POOL_PAD, POOL_WIDTH), F32),
            jax.ShapeDtypeStruct((batch, SUBLANES, CONV_WIDTH), F32),
            jax.ShapeDtypeStruct((batch, SUBLANES, 2 * D_FF), F32),
        ],
        scratch_shapes=[
            pltpu.VMEM((POOL_PAD, POOL_WIDTH), F32),
            pltpu.VMEM((SUBLANES, CONV_WIDTH), F32),
            pltpu.VMEM((SUBLANES, 2 * D_FF), F32),
        ] + tile_scratch + [
            pltpu.VMEM((POOL_WIDTH // LANES, POOL_PAD + TILE, LANES), F32),
            pltpu.VMEM((CONV_WIDTH // LANES, SUBLANES + TILE, LANES), F32),
            pltpu.VMEM((FF_BUFFERS, CHUNK_SLABS, SUBLANES + TILE, LANES), F32),
        ] + act_scratch,
        compiler_params=pltpu.CompilerParams(
            dimension_semantics=("arbitrary", "arbitrary"), vmem_limit_bytes=VMEM_LIMIT_BYTES),
        name="prompt_layer",
    )(slots, x_prompt, *weights)

    sp, sc, sf = state_pool[0], state_conv[0], state_ffn[0]

    def seq_block(a, buffers):
        return pl.BlockSpec((SEQ_PER_TILE,) + a.shape[1:], lambda i: (i, 0, 0),
                            pipeline_mode=pl.Buffered(buffers))

    ys, ps, cs, fs = pl.pallas_call(
        _sample_kernel,
        grid=(dec_batch // SEQ_PER_TILE,),
        in_specs=[slot_spec, seq_block(x_sample, 2), seq_block(sp, 1), seq_block(sc, 1), seq_block(sf, 1)] + w_specs,
        out_specs=[seq_block(x_sample, 2), seq_block(sp, 1), seq_block(sc, 1), seq_block(sf, 1)],
        out_shape=[
            jax.ShapeDtypeStruct(x_sample.shape, F32),
            jax.ShapeDtypeStruct(sp.shape, F32),
            jax.ShapeDtypeStruct(sc.shape, F32),
            jax.ShapeDtypeStruct(sf.shape, F32),
        ],
        scratch_shapes=tile_scratch + act_scratch,
        compiler_params=pltpu.CompilerParams(
            dimension_semantics=("arbitrary",), vmem_limit_bytes=VMEM_LIMIT_BYTES),
        name="sample_layer",
    )(slots, x_sample, sp, sc, sf, *weights)

    return (
        yp,
        ys,
        pp[None, :, POOL_PAD - POOL_HIST:, :],
        cp[None, :, SUBLANES - (CONV_K - 1):, :],
        fp[None, :, SUBLANES - (CONV_K - 1):, :],
        ps[None],
        cs[None],
        fs[None],
    )
```

```python
import jax
import jax.numpy as jnp
from jax import lax
from jax.experimental import pallas as pl
from jax.experimental.pallas import tpu as pltpu

D_MODEL = 1024
POOL_WIDTH = 512
CONV_WIDTH = 512
POOL_WINDOWS = (2, 4, 8, 16)
GROUP_DIM = 128
POOL_HIST = 15
CONV_K = 3
D_FF = 2816
IN_WIDTH = POOL_WIDTH + 3 * CONV_WIDTH
EPS = 1e-6

LANES = 128
SUBLANES = 8
TILE = 512
FF_CHUNK = 256
N_CHUNKS = D_FF // FF_CHUNK
CHUNK_SLABS = 2 * FF_CHUNK // LANES
FF_BUFFERS = 4
POOL_PAD = 16
DEC_SEQ = 8
SEQ_PER_TILE = TILE // DEC_SEQ
VMEM_LIMIT_BYTES = 56 * 1024 * 1024

F32 = jnp.float32
BF16 = jnp.bfloat16


def _rmsnorm(x, g):
    ms = jnp.sum(x * x, axis=-1, keepdims=True) * (1.0 / D_MODEL)
    return (x * lax.rsqrt(ms + EPS)) * g


def _silu(x):
    return x * (1.0 / (1.0 + jnp.exp(-x)))


def _lanes(i):
    return slice(i * LANES, (i + 1) * LANES)


def _ff_cols(c):
    g0 = c * FF_CHUNK
    v0 = D_FF + c * FF_CHUNK
    return slice(g0, g0 + FF_CHUNK), slice(v0, v0 + FF_CHUNK)


def _ff_slab_cols(c, s):
    gcols, vcols = _ff_cols(c)
    half = FF_CHUNK // LANES
    base = gcols.start if s < half else vcols.start
    off = (s % half) * LANES
    return slice(base + off, base + off + LANES)


def _taps(cw, s2, s1, s0):
    return cw[0:1, :] * s2 + cw[1:2, :] * s1 + cw[2:3, :] * s0


def _proj(h_ref, w_ref, cols):
    return jnp.dot(h_ref[...], w_ref[:, cols], preferred_element_type=F32)


def _conv_ffn(slot_ref, h_ref, act_refs, wup_ref, fcw_ref, wdown_ref, put_up, conv_up):
    def up_proj(c):
        gcols, vcols = _ff_cols(c)
        w_c = jnp.concatenate([wup_ref[:, gcols], wup_ref[:, vcols]], axis=1)
        return jnp.dot(h_ref[...], w_c, preferred_element_type=F32)

    put_up(0, up_proj(0))
    for c in range(N_CHUNKS):
        if c + 1 < N_CHUNKS:
            put_up(c + 1, up_proj(c + 1))
        gcols, vcols = _ff_cols(c)
        cw = jnp.concatenate([fcw_ref[:, gcols], fcw_ref[:, vcols]], axis=1)
        upc = conv_up(c, cw)
        act = _silu(upc[:, :FF_CHUNK]) * upc[:, FF_CHUNK:]
        act_refs[c][slot_ref[0]] = act.astype(BF16)

    down = None
    for c in range(N_CHUNKS):
        gcols, _ = _ff_cols(c)
        part = jnp.dot(act_refs[c][slot_ref[1]], wdown_ref[gcols, :], preferred_element_type=F32)
        down = part if down is None else down + part
    return down


def _prompt_kernel(slot_ref, x_ref, g1_ref, win_ref, wgrp_ref, scale_ref, cw_ref, wout_ref, g2_ref, wup_ref,
                   fcw_ref, wdown_ref, gf_ref,
                   y_ref, po_ref, co_ref, fo_ref,
                   hu_ref, hc_ref, hf_ref, h_ref, mix_ref, ubuf, cbuf, fbuf, *act_refs):
    j = pl.program_id(1)

    @pl.when(j == 0)
    def _():
        hu_ref[...] = jnp.zeros_like(hu_ref)
        hc_ref[...] = jnp.zeros_like(hc_ref)
        hf_ref[...] = jnp.zeros_like(hf_ref)

    h_ref[...] = _rmsnorm(x_ref[...], g1_ref[...]).astype(BF16)

    u = _proj(h_ref, win_ref, slice(0, POOL_WIDTH))
    for g in range(len(POOL_WINDOWS)):
        ubuf[g, 0:POOL_PAD, :] = hu_ref[:, _lanes(g)]
        ubuf[g, POOL_PAD:, :] = u[:, _lanes(g)]
        hu_ref[:, _lanes(g)] = u[TILE - POOL_PAD:, _lanes(g)]
    po_ref[...] = hu_ref[POOL_PAD - POOL_HIST:, :]

    gc = _proj(h_ref, win_ref, slice(POOL_WIDTH + CONV_WIDTH, POOL_WIDTH + 2 * CONV_WIDTH))
    hv = _proj(h_ref, win_ref, slice(POOL_WIDTH + 2 * CONV_WIDTH, IN_WIDTH))
    v = gc * hv
    for s in range(CONV_WIDTH // LANES):
        cbuf[s, 0:SUBLANES, :] = hc_ref[:, _lanes(s)]
        cbuf[s, SUBLANES:, :] = v[:, _lanes(s)]
        hc_ref[:, _lanes(s)] = v[TILE - SUBLANES:, _lanes(s)]
    co_ref[...] = hc_ref[SUBLANES - (CONV_K - 1):, :]
    gb = _proj(h_ref, win_ref, slice(POOL_WIDTH, POOL_WIDTH + CONV_WIDTH))

    pos = lax.broadcasted_iota(jnp.int32, (TILE, GROUP_DIM), 0) + j * TILE
    for g, win in enumerate(POOL_WINDOWS):
        cur = ubuf[g, POOL_PAD:, :]
        s = cur
        for k in range(1, win):
            s = s + ubuf[g, POOL_PAD - k:POOL_PAD - k + TILE, :]
        cnt = jnp.minimum(pos + 1, win).astype(F32)
        d = s / cnt - cur
        y = jnp.dot(d.astype(BF16), wgrp_ref[g], preferred_element_type=F32) * scale_ref[:, _lanes(g)]
        mix_ref[:, _lanes(g)] = y.astype(BF16)

    for s in range(CONV_WIDTH // LANES):
        taps = [cbuf[s, SUBLANES - k:SUBLANES - k + TILE, :] for k in (2, 1, 0)]
        yc = gb[:, _lanes(s)] * _taps(cw_ref[:, _lanes(s)], *taps)
        mix_ref[:, POOL_WIDTH + s * LANES:POOL_WIDTH + (s + 1) * LANES] = yc.astype(BF16)

    x1 = x_ref[...] + jnp.dot(mix_ref[...], wout_ref[...], preferred_element_type=F32)
    h_ref[...] = _rmsnorm(x1, g2_ref[...]).astype(BF16)

    def put_up(c, up):
        buf = fbuf.at[c % FF_BUFFERS]
        for s in range(CHUNK_SLABS):
            cols = _ff_slab_cols(c, s)
            buf[s, 0:SUBLANES, :] = hf_ref[:, cols]
            buf[s, SUBLANES:, :] = up[:, _lanes(s)]
            hf_ref[:, cols] = up[TILE - SUBLANES:, _lanes(s)]

    def conv_up(c, cw):
        buf = fbuf.at[c % FF_BUFFERS]
        outs = []
        for s in range(CHUNK_SLABS):
            taps = [buf[s, SUBLANES - k:SUBLANES - k + TILE, :] for k in (2, 1, 0)]
            outs.append(_taps(cw[:, _lanes(s)], *taps))
        return jnp.concatenate(outs, axis=1)

    x2 = x1 + _conv_ffn(slot_ref, h_ref, act_refs, wup_ref, fcw_ref, wdown_ref, put_up, conv_up)
    fo_ref[...] = hf_ref[SUBLANES - (CONV_K - 1):, :]
    y_ref[...] = _rmsnorm(x2, gf_ref[...])


def _sample_kernel(slot_ref, x_ref, sp_ref, sc_ref, sf_ref, g1_ref, win_ref, wgrp_ref, scale_ref, cw_ref,
                   wout_ref, g2_ref, wup_ref, fcw_ref, wdown_ref, gf_ref,
                   y_ref, po_ref, co_ref, fo_ref,
                   h_ref, mix_ref, xs_ref, ys_ref, *act_refs):
    n = SEQ_PER_TILE
    col_blocks = D_MODEL // LANES

    def step_rows(a, t0, t1):
        return a[t0 * n:t1 * n]

    x_seq = x_ref[...].reshape(TILE, D_MODEL)
    for c in range(col_blocks):
        xs_ref[c] = x_seq[:, _lanes(c)]
    x = jnp.concatenate(
        [jnp.concatenate([xs_ref[c, pl.ds(t, n, stride=DEC_SEQ), :] for t in range(DEC_SEQ)], axis=0)
         for c in range(col_blocks)], axis=1)
    h_ref[...] = _rmsnorm(x, g1_ref[...]).astype(BF16)

    u = _proj(h_ref, win_ref, slice(0, POOL_WIDTH))
    for g, win in enumerate(POOL_WINDOWS):
        cols = _lanes(g)
        cur = u[:, cols]
        ext = [sp_ref[r, :, cols] for r in range(POOL_HIST)] + [step_rows(cur, t, t + 1) for t in range(DEC_SEQ)]
        sums = []
        for t in range(DEC_SEQ):
            s = ext[POOL_HIST + t]
            for k in range(1, win):
                s = s + ext[POOL_HIST + t - k]
            sums.append(s)
        d = jnp.concatenate(sums, axis=0) * (1.0 / win) - cur
        y = jnp.dot(d.astype(BF16), wgrp_ref[g], preferred_element_type=F32) * scale_ref[:, cols]
        mix_ref[:, cols] = y.astype(BF16)
        for r in range(POOL_HIST):
            po_ref[r, :, cols] = ext[DEC_SEQ + r]

    def conv3(cur, h0, h1, cw):
        s2 = jnp.concatenate([h0, h1, step_rows(cur, 0, DEC_SEQ - 2)], axis=0)
        s1 = jnp.concatenate([h1, step_rows(cur, 0, DEC_SEQ - 1)], axis=0)
        return _taps(cw, s2, s1, cur)

    gc = _proj(h_ref, win_ref, slice(POOL_WIDTH + CONV_WIDTH, POOL_WIDTH + 2 * CONV_WIDTH))
    hv = _proj(h_ref, win_ref, slice(POOL_WIDTH + 2 * CONV_WIDTH, IN_WIDTH))
    v = gc * hv
    gb = _proj(h_ref, win_ref, slice(POOL_WIDTH, POOL_WIDTH + CONV_WIDTH))
    yc = gb * conv3(v, sc_ref[:, 0, :], sc_ref[:, 1, :], cw_ref[...])
    mix_ref[:, POOL_WIDTH:] = yc.astype(BF16)
    co_ref[:, 0, :] = step_rows(v, DEC_SEQ - 2, DEC_SEQ - 1)
    co_ref[:, 1, :] = step_rows(v, DEC_SEQ - 1, DEC_SEQ)

    x1 = x + jnp.dot(mix_ref[...], wout_ref[...], preferred_element_type=F32)
    h_ref[...] = _rmsnorm(x1, g2_ref[...]).astype(BF16)

    ups = {}

    def put_up(c, up):
        ups[c] = up

    def conv_up(c, cw):
        gcols, vcols = _ff_cols(c)
        up = ups.pop(c)
        for r in range(CONV_K - 1):
            last = step_rows(up, DEC_SEQ - 2 + r, DEC_SEQ - 1 + r)
            fo_ref[:, r, gcols] = last[:, :FF_CHUNK]
            fo_ref[:, r, vcols] = last[:, FF_CHUNK:]
        h0, h1 = (jnp.concatenate([sf_ref[:, r, gcols], sf_ref[:, r, vcols]], axis=1) for r in range(CONV_K - 1))
        return conv3(up, h0, h1, cw)

    x2 = x1 + _conv_ffn(slot_ref, h_ref, act_refs, wup_ref, fcw_ref, wdown_ref, put_up, conv_up)
    y = _rmsnorm(x2, gf_ref[...])
    for c in range(col_blocks):
        for t in range(DEC_SEQ):
            ys_ref[c, pl.ds(t, n, stride=DEC_SEQ), :] = step_rows(y, t, t + 1)[:, _lanes(c)]
    y_seq = jnp.concatenate([ys_ref[c] for c in range(col_blocks)], axis=1)
    y_ref[...] = y_seq.reshape(n, DEC_SEQ, D_MODEL)


def _whole(a):
    nd = a.ndim
    return pl.BlockSpec(a.shape, lambda *_: (0,) * nd, pipeline_mode=pl.Buffered(1))


def kernel(x_prompt, x_sample, state_pool, state_conv, state_ffn, norm1_g, w_in, w_pool_grp, pool_scale,
           conv_w, w_out, norm2_g, w_up, ffn_conv_w, w_down, final_g):
    depth = w_in.shape[0]
    assert depth == 1
    batch, seq, _ = x_prompt.shape
    dec_batch, dec_seq, _ = x_sample.shape
    assert seq % TILE == 0 and dec_seq == DEC_SEQ and dec_batch % SEQ_PER_TILE == 0

    weights = (
        norm1_g[0].reshape(1, D_MODEL),
        w_in[0].astype(BF16),
        w_pool_grp[0].astype(BF16),
        pool_scale[0].reshape(1, POOL_WIDTH),
        conv_w[0],
        w_out[0].astype(BF16),
        norm2_g[0].reshape(1, D_MODEL),
        w_up[0].astype(BF16),
        ffn_conv_w[0],
        w_down[0].astype(BF16),
        final_g.reshape(1, D_MODEL),
    )
    w_specs = [_whole(w) for w in weights]
    slots = jnp.zeros((2,), jnp.int32)
    slot_spec = pl.BlockSpec(memory_space=pltpu.SMEM)
    tile_scratch = [
        pltpu.VMEM((TILE, D_MODEL), BF16),
        pltpu.VMEM((TILE, D_MODEL), BF16),
    ]
    act_scratch = [pltpu.VMEM((1, TILE, FF_CHUNK), BF16)] * N_CHUNKS

    n_tiles = seq // TILE
    state_rows = (POOL_HIST, CONV_K - 1, CONV_K - 1)
    state_widths = (POOL_WIDTH, CONV_WIDTH, 2 * D_FF)
    yp, pp, cp, fp = pl.pallas_call(
        _prompt_kernel,
        grid=(batch, n_tiles),
        in_specs=[slot_spec, pl.BlockSpec((None, TILE, D_MODEL), lambda b, j: (b, j, 0))] + w_specs,
        out_specs=[pl.BlockSpec((None, TILE, D_MODEL), lambda b, j: (b, j, 0))] + [
            pl.BlockSpec((None, r, w), lambda b, j: (b, 0, 0)) for r, w in zip(state_rows, state_widths)],
        out_shape=[jax.ShapeDtypeStruct((batch, seq, D_MODEL), F32)] + [
            jax.ShapeDtypeStruct((batch, r, w), F32) for r, w in zip(state_rows, state_widths)],
        scratch_shapes=[
            pltpu.VMEM((POOL_PAD, POOL_WIDTH), F32),
            pltpu.VMEM((SUBLANES, CONV_WIDTH), F32),
            pltpu.VMEM((SUBLANES, 2 * D_FF), F32),
        ] + tile_scratch + [
            pltpu.VMEM((POOL_WIDTH // LANES, POOL_PAD + TILE, LANES), F32),
            pltpu.VMEM((CONV_WIDTH // LANES, SUBLANES + TILE, LANES), F32),
            pltpu.VMEM((FF_BUFFERS, CHUNK_SLABS, SUBLANES + TILE, LANES), F32),
        ] + act_scratch,
        compiler_params=pltpu.CompilerParams(
            dimension_semantics=("arbitrary", "arbitrary"), vmem_limit_bytes=VMEM_LIMIT_BYTES),
        name="prompt_layer",
    )(slots, x_prompt, *weights)

    sp = jnp.transpose(state_pool[0], (1, 0, 2))
    sc, sf = state_conv[0], state_ffn[0]

    def seq_block(a, buffers):
        return pl.BlockSpec((SEQ_PER_TILE,) + a.shape[1:], lambda i: (i, 0, 0),
                            pipeline_mode=pl.Buffered(buffers))

    pool_block = pl.BlockSpec((POOL_HIST, SEQ_PER_TILE, POOL_WIDTH), lambda i: (0, i, 0),
                              pipeline_mode=pl.Buffered(1))
    slab = pltpu.VMEM((D_MODEL // LANES, TILE, LANES), F32)

    ys, ps, cs, fs = pl.pallas_call(
        _sample_kernel,
        grid=(dec_batch // SEQ_PER_TILE,),
        in_specs=[slot_spec, seq_block(x_sample, 2), pool_block, seq_block(sc, 1), seq_block(sf, 1)] + w_specs,
        out_specs=[seq_block(x_sample, 2), pool_block, seq_block(sc, 1), seq_block(sf, 1)],
        out_shape=[
            jax.ShapeDtypeStruct(x_sample.shape, F32),
            jax.ShapeDtypeStruct(sp.shape, F32),
            jax.ShapeDtypeStruct(sc.shape, F32),
            jax.ShapeDtypeStruct(sf.shape, F32),
        ],
        scratch_shapes=tile_scratch + [slab, slab] + act_scratch,
        compiler_params=pltpu.CompilerParams(
            dimension_semantics=("arbitrary",), vmem_limit_bytes=VMEM_LIMIT_BYTES),
        name="sample_layer",
    )(slots, x_sample, sp, sc, sf, *weights)

    return (yp, ys, pp[None], cp[None], fp[None], jnp.transpose(ps, (1, 0, 2))[None], cs[None], fs[None])
```

```python
import functools

import jax
import jax.numpy as jnp
from jax import lax
from jax.experimental import pallas as pl
from jax.experimental.pallas import tpu as pltpu

D_MODEL = 1024
POOL_WIDTH = 512
CONV_WIDTH = 512
POOL_WINDOWS = (2, 4, 8, 16)
GROUP_DIM = 128
POOL_HIST = 15
CONV_K = 3
D_FF = 2816
IN_WIDTH = POOL_WIDTH + 3 * CONV_WIDTH
EPS = 1e-6

LANES = 128
SUBLANES = 8
TILE = 512
FF_CHUNK = 256
N_CHUNKS = D_FF // FF_CHUNK
CHUNK_SLABS = 2 * FF_CHUNK // LANES
FF_BUFFERS = 4
POOL_PAD = 16
DEC_SEQ = 8
SEQ_PER_TILE = TILE // DEC_SEQ
STAGE_ROWS = {IN_WIDTH: 64, D_MODEL: 64, 2 * D_FF: 32}
VMEM_LIMIT_BYTES = 58 * 1024 * 1024

F32 = jnp.float32
BF16 = jnp.bfloat16


def _rmsnorm(x, g):
    ms = jnp.sum(x * x, axis=-1, keepdims=True) * (1.0 / D_MODEL)
    return (x * lax.rsqrt(ms + EPS)) * g


def _silu(x):
    return x * (1.0 / (1.0 + jnp.exp(-x)))


def _lanes(i):
    return slice(i * LANES, (i + 1) * LANES)


def _ff_cols(c):
    g0 = c * FF_CHUNK
    v0 = D_FF + c * FF_CHUNK
    return slice(g0, g0 + FF_CHUNK), slice(v0, v0 + FF_CHUNK)


def _ff_slab_cols(c, s):
    gcols, vcols = _ff_cols(c)
    half = FF_CHUNK // LANES
    base = gcols.start if s < half else vcols.start
    off = (s % half) * LANES
    return slice(base + off, base + off + LANES)


def _taps(cw, s2, s1, s0):
    return cw[0:1, :] * s2 + cw[1:2, :] * s1 + cw[2:3, :] * s0


def _proj(h_ref, w_ref, cols):
    return jnp.dot(h_ref[...], w_ref[:, cols], preferred_element_type=F32)


def _conv_ffn(slot_ref, h_ref, act_refs, wup_ref, fcw_ref, wdown_ref, put_up, conv_up):
    def up_proj(c):
        gcols, vcols = _ff_cols(c)
        w_c = jnp.concatenate([wup_ref[:, gcols], wup_ref[:, vcols]], axis=1)
        return jnp.dot(h_ref[...], w_c, preferred_element_type=F32)

    put_up(0, up_proj(0))
    for c in range(N_CHUNKS):
        if c + 1 < N_CHUNKS:
            put_up(c + 1, up_proj(c + 1))
        gcols, vcols = _ff_cols(c)
        cw = jnp.concatenate([fcw_ref[:, gcols], fcw_ref[:, vcols]], axis=1)
        upc = conv_up(c, cw)
        act = _silu(upc[:, :FF_CHUNK]) * upc[:, FF_CHUNK:]
        act_refs[c][slot_ref[0]] = act.astype(BF16)

    down = None
    for c in range(N_CHUNKS):
        gcols, _ = _ff_cols(c)
        part = jnp.dot(act_refs[c][slot_ref[1]], wdown_ref[gcols, :], preferred_element_type=F32)
        down = part if down is None else down + part
    return down


def _cast_weights(srcs, dsts, outs, stages, in_sems, out_sems):
    for k, (src, dst, out) in enumerate(zip(srcs, dsts, outs)):
        stage = stages[src.shape[1]]
        rows = stage.shape[1]
        n_chunks = src.shape[0] // rows

        def fetch(c, slot, src=src, stage=stage, rows=rows):
            return pltpu.make_async_copy(src.at[pl.ds(c * rows, rows), :], stage.at[slot], in_sems.at[slot])

        fetch(0, 0).start()

        def body(c, carry, dst=dst, stage=stage, rows=rows, n_chunks=n_chunks, fetch=fetch):
            slot = lax.rem(c, 2)

            @pl.when(c + 1 < n_chunks)
            def _():
                fetch(c + 1, 1 - slot).start()

            fetch(c, slot).wait()
            dst[pl.ds(pl.multiple_of(c * rows, rows), rows), :] = stage[slot].astype(BF16)
            return carry

        lax.fori_loop(0, n_chunks, body, 0)
        pltpu.make_async_copy(dst, out, out_sems.at[k]).start()


def _prompt_kernel(n_tiles, slot_ref, x_ref, g1_ref, win_hbm, wgrp_ref, scale_ref, cw_ref, wout_hbm, g2_ref,
                   wup_hbm, fcw_ref, wdown_hbm, gf_ref,
                   y_ref, po_ref, co_ref, fo_ref, win_out, wout_out, wup_out, wdown_out,
                   win_ref, wout_ref, wup_ref, wdown_ref, stage_in, stage_mid, stage_up, in_sems, out_sems,
                   hu_ref, hc_ref, hf_ref, h_ref, mix_ref, ubuf, cbuf, fbuf, *act_refs):
    i = pl.program_id(0)
    big_dsts = (win_ref, wout_ref, wup_ref, wdown_ref)
    big_outs = (win_out, wout_out, wup_out, wdown_out)

    @pl.when(i == 0)
    def _():
        stages = {IN_WIDTH: stage_in, D_MODEL: stage_mid, 2 * D_FF: stage_up}
        _cast_weights((win_hbm, wout_hbm, wup_hbm, wdown_hbm), big_dsts, big_outs, stages, in_sems, out_sems)

    @pl.when(i > 0)
    def _():
        _prompt_tile(lax.rem(i - 1, n_tiles), slot_ref, x_ref, g1_ref, win_ref, wgrp_ref, scale_ref, cw_ref,
                     wout_ref, g2_ref, wup_ref, fcw_ref, wdown_ref, gf_ref, y_ref, po_ref, co_ref, fo_ref,
                     hu_ref, hc_ref, hf_ref, h_ref, mix_ref, ubuf, cbuf, fbuf, act_refs)

    @pl.when(i == pl.num_programs(0) - 1)
    def _():
        for k, (dst, out) in enumerate(zip(big_dsts, big_outs)):
            pltpu.make_async_copy(dst, out, out_sems.at[k]).wait()


def _prompt_tile(j, slot_ref, x_ref, g1_ref, win_ref, wgrp_ref, scale_ref, cw_ref, wout_ref, g2_ref, wup_ref,
                 fcw_ref, wdown_ref, gf_ref, y_ref, po_ref, co_ref, fo_ref,
                 hu_ref, hc_ref, hf_ref, h_ref, mix_ref, ubuf, cbuf, fbuf, act_refs):

    @pl.when(j == 0)
    def _():
        hu_ref[...] = jnp.zeros_like(hu_ref)
        hc_ref[...] = jnp.zeros_like(hc_ref)
        hf_ref[...] = jnp.zeros_like(hf_ref)

    h_ref[...] = _rmsnorm(x_ref[...], g1_ref[...]).astype(BF16)

    u = _proj(h_ref, win_ref, slice(0, POOL_WIDTH))
    for g in range(len(POOL_WINDOWS)):
        ubuf[g, 0:POOL_PAD, :] = hu_ref[:, _lanes(g)]
        ubuf[g, POOL_PAD:, :] = u[:, _lanes(g)]
        hu_ref[:, _lanes(g)] = u[TILE - POOL_PAD:, _lanes(g)]
    po_ref[...] = hu_ref[POOL_PAD - POOL_HIST:, :]

    gc = _proj(h_ref, win_ref, slice(POOL_WIDTH + CONV_WIDTH, POOL_WIDTH + 2 * CONV_WIDTH))
    hv = _proj(h_ref, win_ref, slice(POOL_WIDTH + 2 * CONV_WIDTH, IN_WIDTH))
    v = gc * hv
    for s in range(CONV_WIDTH // LANES):
        cbuf[s, 0:SUBLANES, :] = hc_ref[:, _lanes(s)]
        cbuf[s, SUBLANES:, :] = v[:, _lanes(s)]
        hc_ref[:, _lanes(s)] = v[TILE - SUBLANES:, _lanes(s)]
    co_ref[...] = hc_ref[SUBLANES - (CONV_K - 1):, :]
    gb = _proj(h_ref, win_ref, slice(POOL_WIDTH, POOL_WIDTH + CONV_WIDTH))

    pos = lax.broadcasted_iota(jnp.int32, (TILE, GROUP_DIM), 0) + j * TILE
    for g, win in enumerate(POOL_WINDOWS):
        cur = ubuf[g, POOL_PAD:, :]
        s = cur
        for k in range(1, win):
            s = s + ubuf[g, POOL_PAD - k:POOL_PAD - k + TILE, :]
        cnt = jnp.minimum(pos + 1, win).astype(F32)
        d = s / cnt - cur
        y = jnp.dot(d.astype(BF16), wgrp_ref[g], preferred_element_type=F32) * scale_ref[:, _lanes(g)]
        mix_ref[:, _lanes(g)] = y.astype(BF16)

    for s in range(CONV_WIDTH // LANES):
        taps = [cbuf[s, SUBLANES - k:SUBLANES - k + TILE, :] for k in (2, 1, 0)]
        yc = gb[:, _lanes(s)] * _taps(cw_ref[:, _lanes(s)], *taps)
        mix_ref[:, POOL_WIDTH + s * LANES:POOL_WIDTH + (s + 1) * LANES] = yc.astype(BF16)

    x1 = x_ref[...] + jnp.dot(mix_ref[...], wout_ref[...], preferred_element_type=F32)
    h_ref[...] = _rmsnorm(x1, g2_ref[...]).astype(BF16)

    def put_up(c, up):
        buf = fbuf.at[c % FF_BUFFERS]
        for s in range(CHUNK_SLABS):
            cols = _ff_slab_cols(c, s)
            buf[s, 0:SUBLANES, :] = hf_ref[:, cols]
            buf[s, SUBLANES:, :] = up[:, _lanes(s)]
            hf_ref[:, cols] = up[TILE - SUBLANES:, _lanes(s)]

    def conv_up(c, cw):
        buf = fbuf.at[c % FF_BUFFERS]
        outs = []
        for s in range(CHUNK_SLABS):
            taps = [buf[s, SUBLANES - k:SUBLANES - k + TILE, :] for k in (2, 1, 0)]
            outs.append(_taps(cw[:, _lanes(s)], *taps))
        return jnp.concatenate(outs, axis=1)

    x2 = x1 + _conv_ffn(slot_ref, h_ref, act_refs, wup_ref, fcw_ref, wdown_ref, put_up, conv_up)
    fo_ref[...] = hf_ref[SUBLANES - (CONV_K - 1):, :]
    y_ref[...] = _rmsnorm(x2, gf_ref[...])


def _sample_kernel(slot_ref, x_ref, sp_ref, sc_ref, sf_ref, g1_ref, win_ref, wgrp_ref, scale_ref, cw_ref,
                   wout_ref, g2_ref, wup_ref, fcw_ref, wdown_ref, gf_ref,
                   y_ref, po_ref, co_ref, fo_ref,
                   h_ref, mix_ref, xs_ref, ys_ref, *act_refs):
    n = SEQ_PER_TILE
    col_blocks = D_MODEL // LANES

    def step_rows(a, t0, t1):
        return a[t0 * n:t1 * n]

    x_seq = x_ref[...].reshape(TILE, D_MODEL)
    for c in range(col_blocks):
        xs_ref[c] = x_seq[:, _lanes(c)]
    x = jnp.concatenate(
        [jnp.concatenate([xs_ref[c, pl.ds(t, n, stride=DEC_SEQ), :] for t in range(DEC_SEQ)], axis=0)
         for c in range(col_blocks)], axis=1)
    h_ref[...] = _rmsnorm(x, g1_ref[...]).astype(BF16)

    u = _proj(h_ref, win_ref, slice(0, POOL_WIDTH))
    for g, win in enumerate(POOL_WINDOWS):
        cols = _lanes(g)
        cur = u[:, cols]
        ext = [sp_ref[r, :, cols] for r in range(POOL_HIST)] + [step_rows(cur, t, t + 1) for t in range(DEC_SEQ)]
        sums = []
        for t in range(DEC_SEQ):
            s = ext[POOL_HIST + t]
            for k in range(1, win):
                s = s + ext[POOL_HIST + t - k]
            sums.append(s)
        d = jnp.concatenate(sums, axis=0) * (1.0 / win) - cur
        y = jnp.dot(d.astype(BF16), wgrp_ref[g], preferred_element_type=F32) * scale_ref[:, cols]
        mix_ref[:, cols] = y.astype(BF16)
        for r in range(POOL_HIST):
            po_ref[r, :, cols] = ext[DEC_SEQ + r]

    def conv3(cur, h0, h1, cw):
        s2 = jnp.concatenate([h0, h1, step_rows(cur, 0, DEC_SEQ - 2)], axis=0)
        s1 = jnp.concatenate([h1, step_rows(cur, 0, DEC_SEQ - 1)], axis=0)
        return _taps(cw, s2, s1, cur)

    gc = _proj(h_ref, win_ref, slice(POOL_WIDTH + CONV_WIDTH, POOL_WIDTH + 2 * CONV_WIDTH))
    hv = _proj(h_ref, win_ref, slice(POOL_WIDTH + 2 * CONV_WIDTH, IN_WIDTH))
    v = gc * hv
    gb = _proj(h_ref, win_ref, slice(POOL_WIDTH, POOL_WIDTH + CONV_WIDTH))
    yc = gb * conv3(v, sc_ref[:, 0, :], sc_ref[:, 1, :], cw_ref[...])
    mix_ref[:, POOL_WIDTH:] = yc.astype(BF16)
    co_ref[:, 0, :] = step_rows(v, DEC_SEQ - 2, DEC_SEQ - 1)
    co_ref[:, 1, :] = step_rows(v, DEC_SEQ - 1, DEC_SEQ)

    x1 = x + jnp.dot(mix_ref[...], wout_ref[...], preferred_element_type=F32)
    h_ref[...] = _rmsnorm(x1, g2_ref[...]).astype(BF16)

    ups = {}

    def put_up(c, up):
        ups[c] = up

    def conv_up(c, cw):
        gcols, vcols = _ff_cols(c)
        up = ups.pop(c)
        for r in range(CONV_K - 1):
            last = step_rows(up, DEC_SEQ - 2 + r, DEC_SEQ - 1 + r)
            fo_ref[:, r, gcols] = last[:, :FF_CHUNK]
            fo_ref[:, r, vcols] = last[:, FF_CHUNK:]
        h0, h1 = (jnp.concatenate([sf_ref[:, r, gcols], sf_ref[:, r, vcols]], axis=1) for r in range(CONV_K - 1))
        return conv3(up, h0, h1, cw)

    x2 = x1 + _conv_ffn(slot_ref, h_ref, act_refs, wup_ref, fcw_ref, wdown_ref, put_up, conv_up)
    y = _rmsnorm(x2, gf_ref[...])
    for c in range(col_blocks):
        for t in range(DEC_SEQ):
            ys_ref[c, pl.ds(t, n, stride=DEC_SEQ), :] = step_rows(y, t, t + 1)[:, _lanes(c)]
    y_seq = jnp.concatenate([ys_ref[c] for c in range(col_blocks)], axis=1)
    y_ref[...] = y_seq.reshape(n, DEC_SEQ, D_MODEL)


def _whole(a):
    nd = a.ndim
    return pl.BlockSpec(a.shape, lambda *_: (0,) * nd, pipeline_mode=pl.Buffered(1))


def kernel(x_prompt, x_sample, state_pool, state_conv, state_ffn, norm1_g, w_in, w_pool_grp, pool_scale,
           conv_w, w_out, norm2_g, w_up, ffn_conv_w, w_down, final_g):
    depth = w_in.shape[0]
    assert depth == 1
    batch, seq, _ = x_prompt.shape
    dec_batch, dec_seq, _ = x_sample.shape
    assert seq % TILE == 0 and dec_seq == DEC_SEQ and dec_batch % SEQ_PER_TILE == 0

    big_f32 = (w_in[0], w_out[0], w_up[0], w_down[0])
    g1, g2, gf = norm1_g[0].reshape(1, D_MODEL), norm2_g[0].reshape(1, D_MODEL), final_g.reshape(1, D_MODEL)
    wgrp, scale = w_pool_grp[0].astype(BF16), pool_scale[0].reshape(1, POOL_WIDTH)
    cw, fcw = conv_w[0], ffn_conv_w[0]
    slots = jnp.zeros((2,), jnp.int32)
    slot_spec = pl.BlockSpec(memory_space=pltpu.SMEM)
    hbm_spec = pl.BlockSpec(memory_space=pl.ANY)
    tile_scratch = [
        pltpu.VMEM((TILE, D_MODEL), BF16),
        pltpu.VMEM((TILE, D_MODEL), BF16),
    ]
    act_scratch = [pltpu.VMEM((1, TILE, FF_CHUNK), BF16)] * N_CHUNKS

    n_tiles = seq // TILE
    state_rows = (POOL_HIST, CONV_K - 1, CONV_K - 1)
    state_widths = (POOL_WIDTH, CONV_WIDTH, 2 * D_FF)

    def tile_of(i):
        t = jnp.maximum(i - 1, 0)
        return t // n_tiles, t % n_tiles

    x_spec = pl.BlockSpec((None, TILE, D_MODEL), lambda i: (*tile_of(i), 0))
    prompt_in = [(slots, slot_spec), (x_prompt, x_spec), (g1, _whole(g1)), (big_f32[0], hbm_spec),
                 (wgrp, _whole(wgrp)), (scale, _whole(scale)), (cw, _whole(cw)), (big_f32[1], hbm_spec),
                 (g2, _whole(g2)), (big_f32[2], hbm_spec), (fcw, _whole(fcw)), (big_f32[3], hbm_spec),
                 (gf, _whole(gf))]
    yp, pp, cp, fp, *big_bf16 = pl.pallas_call(
        functools.partial(_prompt_kernel, n_tiles),
        grid=(1 + batch * n_tiles,),
        in_specs=[spec for _, spec in prompt_in],
        out_specs=[x_spec] + [
            pl.BlockSpec((None, r, w), lambda i: (tile_of(i)[0], 0, 0)) for r, w in zip(state_rows, state_widths)
        ] + [hbm_spec] * len(big_f32),
        out_shape=[jax.ShapeDtypeStruct((batch, seq, D_MODEL), F32)] + [
            jax.ShapeDtypeStruct((batch, r, w), F32) for r, w in zip(state_rows, state_widths)
        ] + [jax.ShapeDtypeStruct(w.shape, BF16) for w in big_f32],
        scratch_shapes=[pltpu.VMEM(w.shape, BF16) for w in big_f32] + [
            pltpu.VMEM((2, STAGE_ROWS[width], width), F32) for width in (IN_WIDTH, D_MODEL, 2 * D_FF)
        ] + [
            pltpu.SemaphoreType.DMA((2,)),
            pltpu.SemaphoreType.DMA((len(big_f32),)),
            pltpu.VMEM((POOL_PAD, POOL_WIDTH), F32),
            pltpu.VMEM((SUBLANES, CONV_WIDTH), F32),
            pltpu.VMEM((SUBLANES, 2 * D_FF), F32),
        ] + tile_scratch + [
            pltpu.VMEM((POOL_WIDTH // LANES, POOL_PAD + TILE, LANES), F32),
            pltpu.VMEM((CONV_WIDTH // LANES, SUBLANES + TILE, LANES), F32),
            pltpu.VMEM((FF_BUFFERS, CHUNK_SLABS, SUBLANES + TILE, LANES), F32),
        ] + act_scratch,
        compiler_params=pltpu.CompilerParams(
            dimension_semantics=("arbitrary",), vmem_limit_bytes=VMEM_LIMIT_BYTES),
        name="prompt_layer",
    )(*[a for a, _ in prompt_in])
    weights = (g1, big_bf16[0], wgrp, scale, cw, big_bf16[1], g2, big_bf16[2], fcw, big_bf16[3], gf)
    w_specs = [_whole(w) for w in weights]

    sp = jnp.transpose(state_pool[0], (1, 0, 2))
    sc, sf = state_conv[0], state_ffn[0]

    def seq_block(a, buffers):
        return pl.BlockSpec((SEQ_PER_TILE,) + a.shape[1:], lambda i: (i, 0, 0),
                            pipeline_mode=pl.Buffered(buffers))

    pool_block = pl.BlockSpec((POOL_HIST, SEQ_PER_TILE, POOL_WIDTH), lambda i: (0, i, 0),
                              pipeline_mode=pl.Buffered(1))
    slab = pltpu.VMEM((D_MODEL // LANES, TILE, LANES), F32)

    ys, ps, cs, fs = pl.pallas_call(
        _sample_kernel,
        grid=(dec_batch // SEQ_PER_TILE,),
        in_specs=[slot_spec, seq_block(x_sample, 2), pool_block, seq_block(sc, 1), seq_block(sf, 1)] + w_specs,
        out_specs=[seq_block(x_sample, 2), pool_block, seq_block(sc, 1), seq_block(sf, 1)],
        out_shape=[
            jax.ShapeDtypeStruct(x_sample.shape, F32),
            jax.ShapeDtypeStruct(sp.shape, F32),
            jax.ShapeDtypeStruct(sc.shape, F32),
            jax.ShapeDtypeStruct(sf.shape, F32),
        ],
        scratch_shapes=tile_scratch + [slab, slab] + act_scratch,
        compiler_params=pltpu.CompilerParams(
            dimension_semantics=("arbitrary",), vmem_limit_bytes=VMEM_LIMIT_BYTES),
        name="sample_layer",
    )(slots, x_sample, sp, sc, sf, *weights)

    return (yp, ys, pp[None], cp[None], fp[None], jnp.transpose(ps, (1, 0, 2))[None], cs[None], fs[None])
```

```python
import functools

import jax
import jax.numpy as jnp
from jax import lax
from jax.experimental import pallas as pl
from jax.experimental.pallas import tpu as pltpu

D_MODEL = 1024
POOL_WIDTH = 512
CONV_WIDTH = 512
POOL_WINDOWS = (2, 4, 8, 16)
GROUP_DIM = 128
POOL_HIST = 15
CONV_K = 3
D_FF = 2816
IN_WIDTH = POOL_WIDTH + 3 * CONV_WIDTH
EPS = 1e-6

LANES = 128
SUBLANES = 8
TILE = 512
FF_CHUNK = 256
N_CHUNKS = D_FF // FF_CHUNK
CHUNK_SLABS = 2 * FF_CHUNK // LANES
FF_BUFFERS = 4
POOL_PAD = 16
DEC_SEQ = 8
SEQ_PER_TILE = TILE // DEC_SEQ
STAGE_ROWS, STAGE_COLS = 256, 512
STAGE_DEPTH = 8
VMEM_LIMIT_BYTES = 58 * 1024 * 1024

F32 = jnp.float32
BF16 = jnp.bfloat16


def _rmsnorm(x, g):
    ms = jnp.sum(x * x, axis=-1, keepdims=True) * (1.0 / D_MODEL)
    return (x * lax.rsqrt(ms + EPS)) * g


def _silu(x):
    return x * (1.0 / (1.0 + jnp.exp(-x)))


def _lanes(i):
    return slice(i * LANES, (i + 1) * LANES)


def _ff_cols(c):
    g0 = c * FF_CHUNK
    v0 = D_FF + c * FF_CHUNK
    return slice(g0, g0 + FF_CHUNK), slice(v0, v0 + FF_CHUNK)


def _ff_slab_cols(c, s):
    gcols, vcols = _ff_cols(c)
    half = FF_CHUNK // LANES
    base = gcols.start if s < half else vcols.start
    off = (s % half) * LANES
    return slice(base + off, base + off + LANES)


def _taps(cw, s2, s1, s0):
    return cw[0:1, :] * s2 + cw[1:2, :] * s1 + cw[2:3, :] * s0


def _proj(h_ref, w_ref, cols):
    return jnp.dot(h_ref[...], w_ref[:, cols], preferred_element_type=F32)


def _conv_ffn(slot_ref, h_ref, act_refs, wup_ref, fcw_ref, wdown_ref, put_up, conv_up):
    def up_proj(c):
        gcols, vcols = _ff_cols(c)
        w_c = jnp.concatenate([wup_ref[:, gcols], wup_ref[:, vcols]], axis=1)
        return jnp.dot(h_ref[...], w_c, preferred_element_type=F32)

    put_up(0, up_proj(0))
    for c in range(N_CHUNKS):
        if c + 1 < N_CHUNKS:
            put_up(c + 1, up_proj(c + 1))
        gcols, vcols = _ff_cols(c)
        cw = jnp.concatenate([fcw_ref[:, gcols], fcw_ref[:, vcols]], axis=1)
        upc = conv_up(c, cw)
        act = _silu(upc[:, :FF_CHUNK]) * upc[:, FF_CHUNK:]
        act_refs[c][slot_ref[0]] = act.astype(BF16)

    down = None
    for c in range(N_CHUNKS):
        gcols, _ = _ff_cols(c)
        part = jnp.dot(act_refs[c][slot_ref[1]], wdown_ref[gcols, :], preferred_element_type=F32)
        down = part if down is None else down + part
    return down


def _cast_weights(srcs, dsts, outs, stage, in_sems, out_sems):
    depth = stage.shape[0]
    pieces = [(k, r, c) for k, src in enumerate(srcs)
              for r in range(0, src.shape[0], STAGE_ROWS) for c in range(0, src.shape[1], STAGE_COLS)]

    def fetch(p):
        k, r, c = pieces[p]
        window = srcs[k].at[r:r + STAGE_ROWS, c:c + STAGE_COLS]
        return pltpu.make_async_copy(window, stage.at[p % depth], in_sems.at[p % depth])

    for p in range(depth):
        fetch(p).start()
    for p, (k, r, c) in enumerate(pieces):
        fetch(p).wait()
        dsts[k][r:r + STAGE_ROWS, c:c + STAGE_COLS] = stage[p % depth].astype(BF16)
        if p + depth < len(pieces):
            fetch(p + depth).start()
        if p + 1 == len(pieces) or pieces[p + 1][0] != k:
            pltpu.make_async_copy(dsts[k], outs[k], out_sems.at[k]).start()


def _prompt_kernel(n_tiles, slot_ref, x_ref, g1_ref, win_hbm, wgrp_ref, scale_ref, cw_ref, wout_hbm, g2_ref,
                   wup_hbm, fcw_ref, wdown_hbm, gf_ref,
                   y_ref, po_ref, co_ref, fo_ref, win_out, wout_out, wup_out, wdown_out,
                   win_ref, wout_ref, wup_ref, wdown_ref, stage_ref, in_sems, out_sems,
                   hu_ref, hc_ref, hf_ref, h_ref, mix_ref, ubuf, cbuf, fbuf, *act_refs):
    i = pl.program_id(0)
    big_dsts = (win_ref, wout_ref, wup_ref, wdown_ref)
    big_outs = (win_out, wout_out, wup_out, wdown_out)

    @pl.when(i == 0)
    def _():
        _cast_weights((win_hbm, wout_hbm, wup_hbm, wdown_hbm), big_dsts, big_outs, stage_ref, in_sems, out_sems)

    @pl.when(i > 0)
    def _():
        _prompt_tile(lax.rem(i - 1, n_tiles), slot_ref, x_ref, g1_ref, win_ref, wgrp_ref, scale_ref, cw_ref,
                     wout_ref, g2_ref, wup_ref, fcw_ref, wdown_ref, gf_ref, y_ref, po_ref, co_ref, fo_ref,
                     hu_ref, hc_ref, hf_ref, h_ref, mix_ref, ubuf, cbuf, fbuf, act_refs)

    @pl.when(i == pl.num_programs(0) - 1)
    def _():
        for k, (dst, out) in enumerate(zip(big_dsts, big_outs)):
            pltpu.make_async_copy(dst, out, out_sems.at[k]).wait()


def _prompt_tile(j, slot_ref, x_ref, g1_ref, win_ref, wgrp_ref, scale_ref, cw_ref, wout_ref, g2_ref, wup_ref,
                 fcw_ref, wdown_ref, gf_ref, y_ref, po_ref, co_ref, fo_ref,
                 hu_ref, hc_ref, hf_ref, h_ref, mix_ref, ubuf, cbuf, fbuf, act_refs):

    @pl.when(j == 0)
    def _():
        hu_ref[...] = jnp.zeros_like(hu_ref)
        hc_ref[...] = jnp.zeros_like(hc_ref)
        hf_ref[...] = jnp.zeros_like(hf_ref)

    h_ref[...] = _rmsnorm(x_ref[...], g1_ref[...]).astype(BF16)

    u = _proj(h_ref, win_ref, slice(0, POOL_WIDTH))
    for g in range(len(POOL_WINDOWS)):
        ubuf[g, 0:POOL_PAD, :] = hu_ref[:, _lanes(g)]
        ubuf[g, POOL_PAD:, :] = u[:, _lanes(g)]
        hu_ref[:, _lanes(g)] = u[TILE - POOL_PAD:, _lanes(g)]
    po_ref[...] = hu_ref[POOL_PAD - POOL_HIST:, :]

    gc = _proj(h_ref, win_ref, slice(POOL_WIDTH + CONV_WIDTH, POOL_WIDTH + 2 * CONV_WIDTH))
    hv = _proj(h_ref, win_ref, slice(POOL_WIDTH + 2 * CONV_WIDTH, IN_WIDTH))
    v = gc * hv
    for s in range(CONV_WIDTH // LANES):
        cbuf[s, 0:SUBLANES, :] = hc_ref[:, _lanes(s)]
        cbuf[s, SUBLANES:, :] = v[:, _lanes(s)]
        hc_ref[:, _lanes(s)] = v[TILE - SUBLANES:, _lanes(s)]
    co_ref[...] = hc_ref[SUBLANES - (CONV_K - 1):, :]
    gb = _proj(h_ref, win_ref, slice(POOL_WIDTH, POOL_WIDTH + CONV_WIDTH))

    pos = lax.broadcasted_iota(jnp.int32, (TILE, GROUP_DIM), 0) + j * TILE
    for g, win in enumerate(POOL_WINDOWS):
        cur = ubuf[g, POOL_PAD:, :]
        s = cur
        for k in range(1, win):
            s = s + ubuf[g, POOL_PAD - k:POOL_PAD - k + TILE, :]
        cnt = jnp.minimum(pos + 1, win).astype(F32)
        d = s / cnt - cur
        y = jnp.dot(d.astype(BF16), wgrp_ref[g], preferred_element_type=F32) * scale_ref[:, _lanes(g)]
        mix_ref[:, _lanes(g)] = y.astype(BF16)

    for s in range(CONV_WIDTH // LANES):
        taps = [cbuf[s, SUBLANES - k:SUBLANES - k + TILE, :] for k in (2, 1, 0)]
        yc = gb[:, _lanes(s)] * _taps(cw_ref[:, _lanes(s)], *taps)
        mix_ref[:, POOL_WIDTH + s * LANES:POOL_WIDTH + (s + 1) * LANES] = yc.astype(BF16)

    x1 = x_ref[...] + jnp.dot(mix_ref[...], wout_ref[...], preferred_element_type=F32)
    h_ref[...] = _rmsnorm(x1, g2_ref[...]).astype(BF16)

    def put_up(c, up):
        buf = fbuf.at[c % FF_BUFFERS]
        for s in range(CHUNK_SLABS):
            cols = _ff_slab_cols(c, s)
            buf[s, 0:SUBLANES, :] = hf_ref[:, cols]
            buf[s, SUBLANES:, :] = up[:, _lanes(s)]
            hf_ref[:, cols] = up[TILE - SUBLANES:, _lanes(s)]

    def conv_up(c, cw):
        buf = fbuf.at[c % FF_BUFFERS]
        outs = []
        for s in range(CHUNK_SLABS):
            taps = [buf[s, SUBLANES - k:SUBLANES - k + TILE, :] for k in (2, 1, 0)]
            outs.append(_taps(cw[:, _lanes(s)], *taps))
        return jnp.concatenate(outs, axis=1)

    x2 = x1 + _conv_ffn(slot_ref, h_ref, act_refs, wup_ref, fcw_ref, wdown_ref, put_up, conv_up)
    fo_ref[...] = hf_ref[SUBLANES - (CONV_K - 1):, :]
    y_ref[...] = _rmsnorm(x2, gf_ref[...])


def _sample_kernel(slot_ref, x_ref, sp_ref, sc_ref, sf_ref, g1_ref, win_ref, wgrp_ref, scale_ref, cw_ref,
                   wout_ref, g2_ref, wup_ref, fcw_ref, wdown_ref, gf_ref,
                   y_ref, po_ref, co_ref, fo_ref,
                   h_ref, mix_ref, xs_ref, ys_ref, *act_refs):
    n = SEQ_PER_TILE
    col_blocks = D_MODEL // LANES

    def step_rows(a, t0, t1):
        return a[t0 * n:t1 * n]

    x_seq = x_ref[...].reshape(TILE, D_MODEL)
    for c in range(col_blocks):
        xs_ref[c] = x_seq[:, _lanes(c)]
    x = jnp.concatenate(
        [jnp.concatenate([xs_ref[c, pl.ds(t, n, stride=DEC_SEQ), :] for t in range(DEC_SEQ)], axis=0)
         for c in range(col_blocks)], axis=1)
    h_ref[...] = _rmsnorm(x, g1_ref[...]).astype(BF16)

    u = _proj(h_ref, win_ref, slice(0, POOL_WIDTH))
    for g, win in enumerate(POOL_WINDOWS):
        cols = _lanes(g)
        cur = u[:, cols]
        ext = [sp_ref[r, :, cols] for r in range(POOL_HIST)] + [step_rows(cur, t, t + 1) for t in range(DEC_SEQ)]
        sums = []
        for t in range(DEC_SEQ):
            s = ext[POOL_HIST + t]
            for k in range(1, win):
                s = s + ext[POOL_HIST + t - k]
            sums.append(s)
        d = jnp.concatenate(sums, axis=0) * (1.0 / win) - cur
        y = jnp.dot(d.astype(BF16), wgrp_ref[g], preferred_element_type=F32) * scale_ref[:, cols]
        mix_ref[:, cols] = y.astype(BF16)
        for r in range(POOL_HIST):
            po_ref[r, :, cols] = ext[DEC_SEQ + r]

    def conv3(cur, h0, h1, cw):
        s2 = jnp.concatenate([h0, h1, step_rows(cur, 0, DEC_SEQ - 2)], axis=0)
        s1 = jnp.concatenate([h1, step_rows(cur, 0, DEC_SEQ - 1)], axis=0)
        return _taps(cw, s2, s1, cur)

    gc = _proj(h_ref, win_ref, slice(POOL_WIDTH + CONV_WIDTH, POOL_WIDTH + 2 * CONV_WIDTH))
    hv = _proj(h_ref, win_ref, slice(POOL_WIDTH + 2 * CONV_WIDTH, IN_WIDTH))
    v = gc * hv
    gb = _proj(h_ref, win_ref, slice(POOL_WIDTH, POOL_WIDTH + CONV_WIDTH))
    yc = gb * conv3(v, sc_ref[:, 0, :], sc_ref[:, 1, :], cw_ref[...])
    mix_ref[:, POOL_WIDTH:] = yc.astype(BF16)
    co_ref[:, 0, :] = step_rows(v, DEC_SEQ - 2, DEC_SEQ - 1)
    co_ref[:, 1, :] = step_rows(v, DEC_SEQ - 1, DEC_SEQ)

    x1 = x + jnp.dot(mix_ref[...], wout_ref[...], preferred_element_type=F32)
    h_ref[...] = _rmsnorm(x1, g2_ref[...]).astype(BF16)

    ups = {}

    def put_up(c, up):
        ups[c] = up

    def conv_up(c, cw):
        gcols, vcols = _ff_cols(c)
        up = ups.pop(c)
        for r in range(CONV_K - 1):
            last = step_rows(up, DEC_SEQ - 2 + r, DEC_SEQ - 1 + r)
            fo_ref[:, r, gcols] = last[:, :FF_CHUNK]
            fo_ref[:, r, vcols] = last[:, FF_CHUNK:]
        h0, h1 = (jnp.concatenate([sf_ref[:, r, gcols], sf_ref[:, r, vcols]], axis=1) for r in range(CONV_K - 1))
        return conv3(up, h0, h1, cw)

    x2 = x1 + _conv_ffn(slot_ref, h_ref, act_refs, wup_ref, fcw_ref, wdown_ref, put_up, conv_up)
    y = _rmsnorm(x2, gf_ref[...])
    for c in range(col_blocks):
        for t in range(DEC_SEQ):
            ys_ref[c, pl.ds(t, n, stride=DEC_SEQ), :] = step_rows(y, t, t + 1)[:, _lanes(c)]
    y_seq = jnp.concatenate([ys_ref[c] for c in range(col_blocks)], axis=1)
    y_ref[...] = y_seq.reshape(n, DEC_SEQ, D_MODEL)


def _whole(a):
    nd = a.ndim
    return pl.BlockSpec(a.shape, lambda *_: (0,) * nd, pipeline_mode=pl.Buffered(1))


def kernel(x_prompt, x_sample, state_pool, state_conv, state_ffn, norm1_g, w_in, w_pool_grp, pool_scale,
           conv_w, w_out, norm2_g, w_up, ffn_conv_w, w_down, final_g):
    depth = w_in.shape[0]
    assert depth == 1
    batch, seq, _ = x_prompt.shape
    dec_batch, dec_seq, _ = x_sample.shape
    assert seq % TILE == 0 and dec_seq == DEC_SEQ and dec_batch % SEQ_PER_TILE == 0
    assert all(w.shape[1] % STAGE_ROWS == 0 and w.shape[2] % STAGE_COLS == 0 for w in (w_in, w_out, w_up, w_down))

    big_f32 = (w_in[0], w_out[0], w_up[0], w_down[0])
    g1, g2, gf = norm1_g[0].reshape(1, D_MODEL), norm2_g[0].reshape(1, D_MODEL), final_g.reshape(1, D_MODEL)
    wgrp, scale = w_pool_grp[0].astype(BF16), pool_scale[0].reshape(1, POOL_WIDTH)
    cw, fcw = conv_w[0], ffn_conv_w[0]
    slots = jnp.zeros((2,), jnp.int32)
    slot_spec = pl.BlockSpec(memory_space=pltpu.SMEM)
    hbm_spec = pl.BlockSpec(memory_space=pl.ANY)
    tile_scratch = [
        pltpu.VMEM((TILE, D_MODEL), BF16),
        pltpu.VMEM((TILE, D_MODEL), BF16),
    ]
    act_scratch = [pltpu.VMEM((1, TILE, FF_CHUNK), BF16)] * N_CHUNKS

    n_tiles = seq // TILE
    state_rows = (POOL_HIST, CONV_K - 1, CONV_K - 1)
    state_widths = (POOL_WIDTH, CONV_WIDTH, 2 * D_FF)

    def tile_of(i):
        t = jnp.maximum(i - 1, 0)
        return t // n_tiles, t % n_tiles

    x_spec = pl.BlockSpec((None, TILE, D_MODEL), lambda i: (*tile_of(i), 0))
    prompt_in = [(slots, slot_spec), (x_prompt, x_spec), (g1, _whole(g1)), (big_f32[0], hbm_spec),
                 (wgrp, _whole(wgrp)), (scale, _whole(scale)), (cw, _whole(cw)), (big_f32[1], hbm_spec),
                 (g2, _whole(g2)), (big_f32[2], hbm_spec), (fcw, _whole(fcw)), (big_f32[3], hbm_spec),
                 (gf, _whole(gf))]
    yp, pp, cp, fp, *big_bf16 = pl.pallas_call(
        functools.partial(_prompt_kernel, n_tiles),
        grid=(1 + batch * n_tiles,),
        in_specs=[spec for _, spec in prompt_in],
        out_specs=[x_spec] + [
            pl.BlockSpec((None, r, w), lambda i: (tile_of(i)[0], 0, 0)) for r, w in zip(state_rows, state_widths)
        ] + [hbm_spec] * len(big_f32),
        out_shape=[jax.ShapeDtypeStruct((batch, seq, D_MODEL), F32)] + [
            jax.ShapeDtypeStruct((batch, r, w), F32) for r, w in zip(state_rows, state_widths)
        ] + [jax.ShapeDtypeStruct(w.shape, BF16) for w in big_f32],
        scratch_shapes=[pltpu.VMEM(w.shape, BF16) for w in big_f32] + [
            pltpu.VMEM((STAGE_DEPTH, STAGE_ROWS, STAGE_COLS), F32),
            pltpu.SemaphoreType.DMA((STAGE_DEPTH,)),
            pltpu.SemaphoreType.DMA((len(big_f32),)),
            pltpu.VMEM((POOL_PAD, POOL_WIDTH), F32),
            pltpu.VMEM((SUBLANES, CONV_WIDTH), F32),
            pltpu.VMEM((SUBLANES, 2 * D_FF), F32),
        ] + tile_scratch + [
            pltpu.VMEM((POOL_WIDTH // LANES, POOL_PAD + TILE, LANES), F32),
            pltpu.VMEM((CONV_WIDTH // LANES, SUBLANES + TILE, LANES), F32),
            pltpu.VMEM((FF_BUFFERS, CHUNK_SLABS, SUBLANES + TILE, LANES), F32),
        ] + act_scratch,
        compiler_params=pltpu.CompilerParams(
            dimension_semantics=("arbitrary",), vmem_limit_bytes=VMEM_LIMIT_BYTES),
        name="prompt_layer",
    )(*[a for a, _ in prompt_in])
    weights = (g1, big_bf16[0], wgrp, scale, cw, big_bf16[1], g2, big_bf16[2], fcw, big_bf16[3], gf)
    w_specs = [_whole(w) for w in weights]

    sp = jnp.transpose(state_pool[0], (1, 0, 2))
    sc, sf = state_conv[0], state_ffn[0]

    def seq_block(a, buffers):
        return pl.BlockSpec((SEQ_PER_TILE,) + a.shape[1:], lambda i: (i, 0, 0),
                            pipeline_mode=pl.Buffered(buffers))

    pool_block = pl.BlockSpec((POOL_HIST, SEQ_PER_TILE, POOL_WIDTH), lambda i: (0, i, 0),
                              pipeline_mode=pl.Buffered(1))
    slab = pltpu.VMEM((D_MODEL // LANES, TILE, LANES), F32)

    ys, ps, cs, fs = pl.pallas_call(
        _sample_kernel,
        grid=(dec_batch // SEQ_PER_TILE,),
        in_specs=[slot_spec, seq_block(x_sample, 2), pool_block, seq_block(sc, 1), seq_block(sf, 1)] + w_specs,
        out_specs=[seq_block(x_sample, 2), pool_block, seq_block(sc, 1), seq_block(sf, 1)],
        out_shape=[
            jax.ShapeDtypeStruct(x_sample.shape, F32),
            jax.ShapeDtypeStruct(sp.shape, F32),
            jax.ShapeDtypeStruct(sc.shape, F32),
            jax.ShapeDtypeStruct(sf.shape, F32),
        ],
        scratch_shapes=tile_scratch + [slab, slab] + act_scratch,
        compiler_params=pltpu.CompilerParams(
            dimension_semantics=("arbitrary",), vmem_limit_bytes=VMEM_LIMIT_BYTES),
        name="sample_layer",
    )(slots, x_sample, sp, sc, sf, *weights)

    return (yp, ys, pp[None], cp[None], fp[None], jnp.transpose(ps, (1, 0, 2))[None], cs[None], fs[None])
```

```python
import functools

import jax
import jax.numpy as jnp
from jax import lax
from jax.experimental import pallas as pl
from jax.experimental.pallas import tpu as pltpu

D_MODEL = 1024
POOL_WIDTH = 512
CONV_WIDTH = 512
POOL_WINDOWS = (2, 4, 8, 16)
GROUP_DIM = 128
POOL_HIST = 15
CONV_K = 3
D_FF = 2816
IN_WIDTH = POOL_WIDTH + 3 * CONV_WIDTH
EPS = 1e-6

LANES = 128
SUBLANES = 8
TILE = 512
DEC_SEQ = 8
RIDE_SEQ = 4
RIDE_ROWS = RIDE_SEQ * DEC_SEQ
ROWS = TILE + RIDE_ROWS
FF_CHUNK = 256
N_CHUNKS = D_FF // FF_CHUNK
CHUNK_SLABS = 2 * FF_CHUNK // LANES
FF_BUFFERS = 3
POOL_PAD = 16
POOL_PITCH = POOL_PAD + DEC_SEQ
CONV_PITCH = SUBLANES + DEC_SEQ
STAGE_ROWS, STAGE_COLS = 256, 512
STAGE_DEPTH = 8
VMEM_LIMIT_BYTES = 58 * 1024 * 1024

F32 = jnp.float32
BF16 = jnp.bfloat16


def _rmsnorm(x, g):
    ms = jnp.sum(x * x, axis=-1, keepdims=True) * (1.0 / D_MODEL)
    return (x * lax.rsqrt(ms + EPS)) * g


def _silu(x):
    return x * (1.0 / (1.0 + jnp.exp(-x)))


def _lanes(i):
    return slice(i * LANES, (i + 1) * LANES)


def _ff_cols(c):
    g0 = c * FF_CHUNK
    v0 = D_FF + c * FF_CHUNK
    return slice(g0, g0 + FF_CHUNK), slice(v0, v0 + FF_CHUNK)


def _ff_slab_cols(c, s):
    gcols, vcols = _ff_cols(c)
    half = FF_CHUNK // LANES
    base = gcols.start if s < half else vcols.start
    off = (s % half) * LANES
    return slice(base + off, base + off + LANES)


def _taps(cw, s2, s1, s0):
    return cw[0:1, :] * s2 + cw[1:2, :] * s1 + cw[2:3, :] * s0


def _proj(h_ref, w_ref, cols):
    return jnp.dot(h_ref[...], w_ref[:, cols], preferred_element_type=F32)


def _conv_ffn(slot_ref, h_ref, act_refs, wup_ref, fcw_ref, wdown_ref, put_up, conv_up):
    def up_proj(c):
        gcols, vcols = _ff_cols(c)
        w_c = jnp.concatenate([wup_ref[:, gcols], wup_ref[:, vcols]], axis=1)
        return jnp.dot(h_ref[...], w_c, preferred_element_type=F32)

    put_up(0, up_proj(0))
    for c in range(N_CHUNKS):
        if c + 1 < N_CHUNKS:
            put_up(c + 1, up_proj(c + 1))
        gcols, vcols = _ff_cols(c)
        cw = jnp.concatenate([fcw_ref[:, gcols], fcw_ref[:, vcols]], axis=1)
        upc = conv_up(c, cw)
        act = _silu(upc[:, :FF_CHUNK]) * upc[:, FF_CHUNK:]
        act_refs[c][slot_ref[0]] = act.astype(BF16)

    down = None
    for c in range(N_CHUNKS):
        gcols, _ = _ff_cols(c)
        part = jnp.dot(act_refs[c][slot_ref[1]], wdown_ref[gcols, :], preferred_element_type=F32)
        down = part if down is None else down + part
    return down


def _cast_weights(srcs, dsts, stage, in_sems):
    depth = stage.shape[0]
    pieces = [(k, r, c) for k, src in enumerate(srcs)
              for r in range(0, src.shape[0], STAGE_ROWS) for c in range(0, src.shape[1], STAGE_COLS)]

    def fetch(p):
        k, r, c = pieces[p]
        window = srcs[k].at[r:r + STAGE_ROWS, c:c + STAGE_COLS]
        return pltpu.make_async_copy(window, stage.at[p % depth], in_sems.at[p % depth])

    for p in range(depth):
        fetch(p).start()
    for p, (k, r, c) in enumerate(pieces):
        fetch(p).wait()
        dsts[k][r:r + STAGE_ROWS, c:c + STAGE_COLS] = stage[p % depth].astype(BF16)
        if p + depth < len(pieces):
            fetch(p + depth).start()


def _layer_kernel(n_tiles, slot_ref, x_ref, xs_ref, sp_ref, sc_ref, sf_ref, g1_ref, win_hbm, wgrp_ref, scale_ref,
                  cw_ref, wout_hbm, g2_ref, wup_hbm, fcw_ref, wdown_hbm, gf_ref,
                  y_ref, po_ref, co_ref, fo_ref, ys_ref, pos_ref, cos_ref, fos_ref,
                  win_ref, wout_ref, wup_ref, wdown_ref, stage_ref, in_sems,
                  hu_ref, hc_ref, hf_ref, h_ref, mix_ref, ubuf, cbuf, fbuf, usub, csub, fsub, *act_refs):
    i = pl.program_id(0)

    @pl.when(i == 0)
    def _():
        _cast_weights((win_hbm, wout_hbm, wup_hbm, wdown_hbm), (win_ref, wout_ref, wup_ref, wdown_ref),
                      stage_ref, in_sems)

    @pl.when(i > 0)
    def _():
        part = lax.rem(i - 1, SUBLANES // RIDE_SEQ)
        for p in range(SUBLANES // RIDE_SEQ):
            @pl.when(part == p)
            def _(p=p):
                for g in range(len(POOL_WINDOWS)):
                    for s in range(RIDE_SEQ):
                        usub[g, s * POOL_PITCH + 1:s * POOL_PITCH + POOL_PAD, :] = sp_ref[:, p * RIDE_SEQ + s, _lanes(g)]

        _tile(lax.rem(i - 1, n_tiles), slot_ref, x_ref, xs_ref, sc_ref, sf_ref, g1_ref, win_ref, wgrp_ref, scale_ref,
              cw_ref, wout_ref, g2_ref, wup_ref, fcw_ref, wdown_ref, gf_ref, y_ref, po_ref, co_ref, fo_ref,
              ys_ref, cos_ref, fos_ref, hu_ref, hc_ref, hf_ref, h_ref, mix_ref, ubuf, cbuf, fbuf, usub, csub, fsub,
              act_refs)

        for p in range(SUBLANES // RIDE_SEQ):
            @pl.when(part == p)
            def _(p=p):
                for g in range(len(POOL_WINDOWS)):
                    for s in range(RIDE_SEQ):
                        lo = s * POOL_PITCH + POOL_PITCH - POOL_HIST
                        pos_ref[:, p * RIDE_SEQ + s, _lanes(g)] = usub[g, lo:lo + POOL_HIST, :]


def _tile(j, slot_ref, x_ref, xs_ref, sc_ref, sf_ref, g1_ref, win_ref, wgrp_ref, scale_ref, cw_ref, wout_ref, g2_ref,
          wup_ref, fcw_ref, wdown_ref, gf_ref, y_ref, po_ref, co_ref, fo_ref, ys_ref, cos_ref, fos_ref,
          hu_ref, hc_ref, hf_ref, h_ref, mix_ref, ubuf, cbuf, fbuf, usub, csub, fsub, act_refs):
    seqs = range(RIDE_SEQ)

    def seq_rows(s):
        return slice(TILE + s * DEC_SEQ, TILE + (s + 1) * DEC_SEQ)

    @pl.when(j == 0)
    def _():
        hu_ref[...] = jnp.zeros_like(hu_ref)
        hc_ref[...] = jnp.zeros_like(hc_ref)
        hf_ref[...] = jnp.zeros_like(hf_ref)

    x = jnp.concatenate([x_ref[...], xs_ref[...].reshape(RIDE_ROWS, D_MODEL)], axis=0)
    h_ref[...] = _rmsnorm(x, g1_ref[...]).astype(BF16)

    u = _proj(h_ref, win_ref, slice(0, POOL_WIDTH))
    for g in range(len(POOL_WINDOWS)):
        ubuf[g, 0:POOL_PAD, :] = hu_ref[:, _lanes(g)]
        ubuf[g, POOL_PAD:, :] = u[:TILE, _lanes(g)]
        hu_ref[:, _lanes(g)] = u[TILE - POOL_PAD:TILE, _lanes(g)]
        for s in seqs:
            usub[g, s * POOL_PITCH + POOL_PAD:(s + 1) * POOL_PITCH, :] = u[seq_rows(s), _lanes(g)]
    po_ref[...] = hu_ref[POOL_PAD - POOL_HIST:, :]

    gc = _proj(h_ref, win_ref, slice(POOL_WIDTH + CONV_WIDTH, POOL_WIDTH + 2 * CONV_WIDTH))
    hv = _proj(h_ref, win_ref, slice(POOL_WIDTH + 2 * CONV_WIDTH, IN_WIDTH))
    v = gc * hv
    for sl in range(CONV_WIDTH // LANES):
        cbuf[sl, 0:SUBLANES, :] = hc_ref[:, _lanes(sl)]
        cbuf[sl, SUBLANES:, :] = v[:TILE, _lanes(sl)]
        hc_ref[:, _lanes(sl)] = v[TILE - SUBLANES:TILE, _lanes(sl)]
        for s in seqs:
            base = s * CONV_PITCH + SUBLANES
            csub[sl, base - (CONV_K - 1):base, :] = sc_ref[s, :, _lanes(sl)]
            csub[sl, base:base + DEC_SEQ, :] = v[seq_rows(s), _lanes(sl)]
            cos_ref[s, :, _lanes(sl)] = v[TILE + (s + 1) * DEC_SEQ - (CONV_K - 1):TILE + (s + 1) * DEC_SEQ, _lanes(sl)]
    co_ref[...] = hc_ref[SUBLANES - (CONV_K - 1):, :]
    gb = _proj(h_ref, win_ref, slice(POOL_WIDTH, POOL_WIDTH + CONV_WIDTH))

    pos = lax.broadcasted_iota(jnp.int32, (TILE, GROUP_DIM), 0) + j * TILE
    for g, win in enumerate(POOL_WINDOWS):
        cur = ubuf[g, POOL_PAD:, :]
        s_p = cur
        for k in range(1, win):
            s_p = s_p + ubuf[g, POOL_PAD - k:POOL_PAD - k + TILE, :]
        cnt = jnp.minimum(pos + 1, win).astype(F32)
        d = [s_p / cnt - cur]
        for s in seqs:
            base = s * POOL_PITCH + POOL_PAD
            cur_s = usub[g, base:base + DEC_SEQ, :]
            s_s = cur_s
            for k in range(1, win):
                s_s = s_s + usub[g, base - k:base - k + DEC_SEQ, :]
            d.append(s_s * (1.0 / win) - cur_s)
        d = jnp.concatenate(d, axis=0)
        y = jnp.dot(d.astype(BF16), wgrp_ref[g], preferred_element_type=F32) * scale_ref[:, _lanes(g)]
        mix_ref[:, _lanes(g)] = y.astype(BF16)

    for sl in range(CONV_WIDTH // LANES):
        taps = [[cbuf[sl, SUBLANES - k:SUBLANES - k + TILE, :]] for k in (2, 1, 0)]
        for s in seqs:
            base = s * CONV_PITCH + SUBLANES
            for t, k in zip(taps, (2, 1, 0)):
                t.append(csub[sl, base - k:base - k + DEC_SEQ, :])
        yc = gb[:, _lanes(sl)] * _taps(cw_ref[:, _lanes(sl)], *[jnp.concatenate(t, axis=0) for t in taps])
        mix_ref[:, POOL_WIDTH + sl * LANES:POOL_WIDTH + (sl + 1) * LANES] = yc.astype(BF16)

    x1 = x + jnp.dot(mix_ref[...], wout_ref[...], preferred_element_type=F32)
    h_ref[...] = _rmsnorm(x1, g2_ref[...]).astype(BF16)

    def put_up(c, up):
        buf, sub = fbuf.at[c % FF_BUFFERS], fsub.at[c % FF_BUFFERS]
        for sl in range(CHUNK_SLABS):
            cols = _ff_slab_cols(c, sl)
            buf[sl, 0:SUBLANES, :] = hf_ref[:, cols]
            buf[sl, SUBLANES:, :] = up[:TILE, _lanes(sl)]
            hf_ref[:, cols] = up[TILE - SUBLANES:TILE, _lanes(sl)]
            for s in seqs:
                base = s * CONV_PITCH + SUBLANES
                sub[sl, base - (CONV_K - 1):base, :] = sf_ref[s, :, cols]
                sub[sl, base:base + DEC_SEQ, :] = up[seq_rows(s), _lanes(sl)]
                fos_ref[s, :, cols] = up[TILE + (s + 1) * DEC_SEQ - (CONV_K - 1):TILE + (s + 1) * DEC_SEQ, _lanes(sl)]

    def conv_up(c, cw):
        buf, sub = fbuf.at[c % FF_BUFFERS], fsub.at[c % FF_BUFFERS]
        outs = []
        for sl in range(CHUNK_SLABS):
            taps = [[buf[sl, SUBLANES - k:SUBLANES - k + TILE, :]] for k in (2, 1, 0)]
            for s in seqs:
                base = s * CONV_PITCH + SUBLANES
                for t, k in zip(taps, (2, 1, 0)):
                    t.append(sub[sl, base - k:base - k + DEC_SEQ, :])
            outs.append(_taps(cw[:, _lanes(sl)], *[jnp.concatenate(t, axis=0) for t in taps]))
        return jnp.concatenate(outs, axis=1)

    x2 = x1 + _conv_ffn(slot_ref, h_ref, act_refs, wup_ref, fcw_ref, wdown_ref, put_up, conv_up)
    fo_ref[...] = hf_ref[SUBLANES - (CONV_K - 1):, :]
    y = _rmsnorm(x2, gf_ref[...])
    y_ref[...] = y[:TILE]
    ys_ref[...] = y[TILE:].reshape(RIDE_SEQ, DEC_SEQ, D_MODEL)


def _whole(a):
    nd = a.ndim
    return pl.BlockSpec(a.shape, lambda *_: (0,) * nd, pipeline_mode=pl.Buffered(1))


def kernel(x_prompt, x_sample, state_pool, state_conv, state_ffn, norm1_g, w_in, w_pool_grp, pool_scale,
           conv_w, w_out, norm2_g, w_up, ffn_conv_w, w_down, final_g):
    depth = w_in.shape[0]
    assert depth == 1
    batch, seq, _ = x_prompt.shape
    dec_batch, dec_seq, _ = x_sample.shape
    n_tiles = seq // TILE
    steps = batch * n_tiles
    assert seq % TILE == 0 and dec_seq == DEC_SEQ and dec_batch == RIDE_SEQ * steps and SUBLANES % RIDE_SEQ == 0
    assert all(w.shape[1] % STAGE_ROWS == 0 and w.shape[2] % STAGE_COLS == 0 for w in (w_in, w_out, w_up, w_down))

    big_f32 = (w_in[0], w_out[0], w_up[0], w_down[0])
    g1, g2, gf = norm1_g[0].reshape(1, D_MODEL), norm2_g[0].reshape(1, D_MODEL), final_g.reshape(1, D_MODEL)
    wgrp, scale = w_pool_grp[0].astype(BF16), pool_scale[0].reshape(1, POOL_WIDTH)
    cw, fcw = conv_w[0], ffn_conv_w[0]
    sp = jnp.transpose(state_pool[0], (1, 0, 2))
    sc, sf = state_conv[0], state_ffn[0]
    slots = jnp.zeros((2,), jnp.int32)
    slot_spec = pl.BlockSpec(memory_space=pltpu.SMEM)
    hbm_spec = pl.BlockSpec(memory_space=pl.ANY)

    def tile_of(i):
        t = jnp.maximum(i - 1, 0)
        return t // n_tiles, t % n_tiles

    def ride_of(i):
        return jnp.maximum(i - 1, 0)

    x_spec = pl.BlockSpec((None, TILE, D_MODEL), lambda i: (*tile_of(i), 0))
    xs_spec = pl.BlockSpec((RIDE_SEQ, DEC_SEQ, D_MODEL), lambda i: (ride_of(i), 0, 0))
    sp_spec = pl.BlockSpec((POOL_HIST, SUBLANES, POOL_WIDTH), lambda i: (0, ride_of(i) // (SUBLANES // RIDE_SEQ), 0))
    sc_spec = pl.BlockSpec((RIDE_SEQ, CONV_K - 1, CONV_WIDTH), lambda i: (ride_of(i), 0, 0))
    sf_spec = pl.BlockSpec((RIDE_SEQ, CONV_K - 1, 2 * D_FF), lambda i: (ride_of(i), 0, 0))
    state_rows = (POOL_HIST, CONV_K - 1, CONV_K - 1)
    state_widths = (POOL_WIDTH, CONV_WIDTH, 2 * D_FF)

    operands = [(slots, slot_spec), (x_prompt, x_spec), (x_sample, xs_spec), (sp, sp_spec), (sc, sc_spec),
                (sf, sf_spec), (g1, _whole(g1)), (big_f32[0], hbm_spec), (wgrp, _whole(wgrp)),
                (scale, _whole(scale)), (cw, _whole(cw)), (big_f32[1], hbm_spec), (g2, _whole(g2)),
                (big_f32[2], hbm_spec), (fcw, _whole(fcw)), (big_f32[3], hbm_spec), (gf, _whole(gf))]
    yp, pp, cp, fp, ys, ps, cs, fs = pl.pallas_call(
        functools.partial(_layer_kernel, n_tiles),
        grid=(1 + steps,),
        in_specs=[spec for _, spec in operands],
        out_specs=[x_spec] + [
            pl.BlockSpec((None, r, w), lambda i: (tile_of(i)[0], 0, 0)) for r, w in zip(state_rows, state_widths)
        ] + [xs_spec, sp_spec, sc_spec, sf_spec],
        out_shape=[jax.ShapeDtypeStruct((batch, seq, D_MODEL), F32)] + [
            jax.ShapeDtypeStruct((batch, r, w), F32) for r, w in zip(state_rows, state_widths)
        ] + [jax.ShapeDtypeStruct(a.shape, F32) for a in (x_sample, sp, sc, sf)],
        scratch_shapes=[pltpu.VMEM(w.shape, BF16) for w in big_f32] + [
            pltpu.VMEM((STAGE_DEPTH, STAGE_ROWS, STAGE_COLS), F32),
            pltpu.SemaphoreType.DMA((STAGE_DEPTH,)),
            pltpu.VMEM((POOL_PAD, POOL_WIDTH), F32),
            pltpu.VMEM((SUBLANES, CONV_WIDTH), F32),
            pltpu.VMEM((SUBLANES, 2 * D_FF), F32),
            pltpu.VMEM((ROWS, D_MODEL), BF16),
            pltpu.VMEM((ROWS, D_MODEL), BF16),
            pltpu.VMEM((POOL_WIDTH // LANES, POOL_PAD + TILE, LANES), F32),
            pltpu.VMEM((CONV_WIDTH // LANES, SUBLANES + TILE, LANES), F32),
            pltpu.VMEM((FF_BUFFERS, CHUNK_SLABS, SUBLANES + TILE, LANES), F32),
            pltpu.VMEM((POOL_WIDTH // LANES, RIDE_SEQ * POOL_PITCH, LANES), F32),
            pltpu.VMEM((CONV_WIDTH // LANES, RIDE_SEQ * CONV_PITCH, LANES), F32),
            pltpu.VMEM((FF_BUFFERS, CHUNK_SLABS, RIDE_SEQ * CONV_PITCH, LANES), F32),
        ] + [pltpu.VMEM((1, ROWS, FF_CHUNK), BF16)] * N_CHUNKS,
        compiler_params=pltpu.CompilerParams(
            dimension_semantics=("arbitrary",), vmem_limit_bytes=VMEM_LIMIT_BYTES),
        name="decoder_layer",
    )(*[a for a, _ in operands])

    return (yp, ys, pp[None], cp[None], fp[None], jnp.transpose(ps, (1, 0, 2))[None], cs[None], fs[None])
```

```python
import functools

import jax
import jax.numpy as jnp
from jax import lax
from jax.experimental import pallas as pl
from jax.experimental.pallas import tpu as pltpu

D_MODEL = 1024
POOL_WIDTH = 512
CONV_WIDTH = 512
POOL_WINDOWS = (2, 4, 8, 16)
GROUP_DIM = 128
POOL_HIST = 15
CONV_K = 3
D_FF = 2816
IN_WIDTH = POOL_WIDTH + 3 * CONV_WIDTH
EPS = 1e-6

LANES = 128
SUBLANES = 8
TILE = 512
DEC_SEQ = 8
RIDE_SEQ = 4
RIDE_ROWS = RIDE_SEQ * DEC_SEQ
ROWS = TILE + RIDE_ROWS
FF_CHUNK = 256
N_CHUNKS = D_FF // FF_CHUNK
CHUNK_SLABS = 2 * FF_CHUNK // LANES
FF_BUFFERS = 3
POOL_PAD = 16
POOL_PITCH = POOL_PAD + DEC_SEQ
CONV_PITCH = SUBLANES + DEC_SEQ
STAGE_ROWS, STAGE_COLS = 256, 512
STAGE_DEPTH = 8
VMEM_LIMIT_BYTES = 58 * 1024 * 1024

F32 = jnp.float32
BF16 = jnp.bfloat16


def _rmsnorm(x, g):
    ms = jnp.sum(x * x, axis=-1, keepdims=True) * (1.0 / D_MODEL)
    return (x * lax.rsqrt(ms + EPS)) * g


def _silu(x):
    return x * (1.0 / (1.0 + jnp.exp(-x)))


def _lanes(i):
    return slice(i * LANES, (i + 1) * LANES)


def _ff_cols(c):
    g0 = c * FF_CHUNK
    v0 = D_FF + c * FF_CHUNK
    return slice(g0, g0 + FF_CHUNK), slice(v0, v0 + FF_CHUNK)


def _ff_slab_cols(c, s):
    gcols, vcols = _ff_cols(c)
    half = FF_CHUNK // LANES
    base = gcols.start if s < half else vcols.start
    off = (s % half) * LANES
    return slice(base + off, base + off + LANES)


def _taps(cw, s2, s1, s0):
    return cw[0:1, :] * s2 + cw[1:2, :] * s1 + cw[2:3, :] * s0


def _proj(h_ref, w_ref, cols):
    return jnp.dot(h_ref[...], w_ref[:, cols], preferred_element_type=F32)


def _conv_ffn(slot_ref, h_ref, act_refs, wup_ref, fcw_ref, wdown_ref, put_up, conv_up):
    def up_proj(c):
        gcols, vcols = _ff_cols(c)
        w_c = jnp.concatenate([wup_ref[:, gcols], wup_ref[:, vcols]], axis=1)
        return jnp.dot(h_ref[...], w_c, preferred_element_type=F32)

    put_up(0, up_proj(0))
    for c in range(N_CHUNKS):
        if c + 1 < N_CHUNKS:
            put_up(c + 1, up_proj(c + 1))
        gcols, vcols = _ff_cols(c)
        cw = jnp.concatenate([fcw_ref[:, gcols], fcw_ref[:, vcols]], axis=1)
        upc = conv_up(c, cw)
        act = _silu(upc[:, :FF_CHUNK]) * upc[:, FF_CHUNK:]
        act_refs[c][slot_ref[0]] = act.astype(BF16)

    down = None
    for c in range(N_CHUNKS):
        gcols, _ = _ff_cols(c)
        part = jnp.dot(act_refs[c][slot_ref[1]], wdown_ref[gcols, :], preferred_element_type=F32)
        down = part if down is None else down + part
    return down


def _cast_weights(srcs, dsts, stage, in_sems):
    depth = stage.shape[0]
    pieces = [(k, r, c) for k, src in enumerate(srcs)
              for r in range(0, src.shape[0], STAGE_ROWS) for c in range(0, src.shape[1], STAGE_COLS)]

    def fetch(p):
        k, r, c = pieces[p]
        window = srcs[k].at[r:r + STAGE_ROWS, c:c + STAGE_COLS]
        return pltpu.make_async_copy(window, stage.at[p % depth], in_sems.at[p % depth])

    for p in range(depth):
        fetch(p).start(priority=p % 2)
    for p, (k, r, c) in enumerate(pieces):
        fetch(p).wait()
        dsts[k][r:r + STAGE_ROWS, c:c + STAGE_COLS] = stage[p % depth].astype(BF16)
        if p + depth < len(pieces):
            fetch(p + depth).start(priority=(p + depth) % 2)


def _layer_kernel(n_tiles, slot_ref, x_ref, xs_ref, sp_ref, sc_ref, sf_ref, g1_ref, win_hbm, wgrp_ref, scale_ref,
                  cw_ref, wout_hbm, g2_ref, wup_hbm, fcw_ref, wdown_hbm, gf_ref,
                  y_ref, po_ref, co_ref, fo_ref, ys_ref, pos_ref, cos_ref, fos_ref,
                  win_ref, wout_ref, wup_ref, wdown_ref, stage_ref, in_sems,
                  hu_ref, hc_ref, hf_ref, h_ref, mix_ref, ubuf, cbuf, fbuf, usub, csub, fsub, *act_refs):
    i = pl.program_id(0)

    @pl.when(i == 0)
    def _():
        _cast_weights((win_hbm, wout_hbm, wup_hbm, wdown_hbm), (win_ref, wout_ref, wup_ref, wdown_ref),
                      stage_ref, in_sems)

    @pl.when(i > 0)
    def _():
        part = lax.rem(i - 1, SUBLANES // RIDE_SEQ)
        for p in range(SUBLANES // RIDE_SEQ):
            @pl.when(part == p)
            def _(p=p):
                for g in range(len(POOL_WINDOWS)):
                    for s in range(RIDE_SEQ):
                        usub[g, s * POOL_PITCH + 1:s * POOL_PITCH + POOL_PAD, :] = sp_ref[:, p * RIDE_SEQ + s, _lanes(g)]

        _tile(lax.rem(i - 1, n_tiles), slot_ref, x_ref, xs_ref, sc_ref, sf_ref, g1_ref, win_ref, wgrp_ref, scale_ref,
              cw_ref, wout_ref, g2_ref, wup_ref, fcw_ref, wdown_ref, gf_ref, y_ref, po_ref, co_ref, fo_ref,
              ys_ref, cos_ref, fos_ref, hu_ref, hc_ref, hf_ref, h_ref, mix_ref, ubuf, cbuf, fbuf, usub, csub, fsub,
              act_refs)

        for p in range(SUBLANES // RIDE_SEQ):
            @pl.when(part == p)
            def _(p=p):
                for g in range(len(POOL_WINDOWS)):
                    for s in range(RIDE_SEQ):
                        lo = s * POOL_PITCH + POOL_PITCH - POOL_HIST
                        pos_ref[:, p * RIDE_SEQ + s, _lanes(g)] = usub[g, lo:lo + POOL_HIST, :]


def _tile(j, slot_ref, x_ref, xs_ref, sc_ref, sf_ref, g1_ref, win_ref, wgrp_ref, scale_ref, cw_ref, wout_ref, g2_ref,
          wup_ref, fcw_ref, wdown_ref, gf_ref, y_ref, po_ref, co_ref, fo_ref, ys_ref, cos_ref, fos_ref,
          hu_ref, hc_ref, hf_ref, h_ref, mix_ref, ubuf, cbuf, fbuf, usub, csub, fsub, act_refs):
    seqs = range(RIDE_SEQ)

    def seq_rows(s):
        return slice(TILE + s * DEC_SEQ, TILE + (s + 1) * DEC_SEQ)

    @pl.when(j == 0)
    def _():
        hu_ref[...] = jnp.zeros_like(hu_ref)
        hc_ref[...] = jnp.zeros_like(hc_ref)
        hf_ref[...] = jnp.zeros_like(hf_ref)

    x = jnp.concatenate([x_ref[...], xs_ref[...].reshape(RIDE_ROWS, D_MODEL)], axis=0)
    h_ref[...] = _rmsnorm(x, g1_ref[...]).astype(BF16)

    u = _proj(h_ref, win_ref, slice(0, POOL_WIDTH))
    for g in range(len(POOL_WINDOWS)):
        ubuf[g, 0:POOL_PAD, :] = hu_ref[:, _lanes(g)]
        ubuf[g, POOL_PAD:, :] = u[:TILE, _lanes(g)]
        hu_ref[:, _lanes(g)] = u[TILE - POOL_PAD:TILE, _lanes(g)]
        for s in seqs:
            usub[g, s * POOL_PITCH + POOL_PAD:(s + 1) * POOL_PITCH, :] = u[seq_rows(s), _lanes(g)]
    po_ref[...] = hu_ref[POOL_PAD - POOL_HIST:, :]

    gc = _proj(h_ref, win_ref, slice(POOL_WIDTH + CONV_WIDTH, POOL_WIDTH + 2 * CONV_WIDTH))
    hv = _proj(h_ref, win_ref, slice(POOL_WIDTH + 2 * CONV_WIDTH, IN_WIDTH))
    v = gc * hv
    for sl in range(CONV_WIDTH // LANES):
        cbuf[sl, 0:SUBLANES, :] = hc_ref[:, _lanes(sl)]
        cbuf[sl, SUBLANES:, :] = v[:TILE, _lanes(sl)]
        hc_ref[:, _lanes(sl)] = v[TILE - SUBLANES:TILE, _lanes(sl)]
        for s in seqs:
            base = s * CONV_PITCH + SUBLANES
            csub[sl, base - (CONV_K - 1):base, :] = sc_ref[s, :, _lanes(sl)]
            csub[sl, base:base + DEC_SEQ, :] = v[seq_rows(s), _lanes(sl)]
            cos_ref[s, :, _lanes(sl)] = v[TILE + (s + 1) * DEC_SEQ - (CONV_K - 1):TILE + (s + 1) * DEC_SEQ, _lanes(sl)]
    co_ref[...] = hc_ref[SUBLANES - (CONV_K - 1):, :]
    gb = _proj(h_ref, win_ref, slice(POOL_WIDTH, POOL_WIDTH + CONV_WIDTH))

    pos = lax.broadcasted_iota(jnp.int32, (TILE, GROUP_DIM), 0) + j * TILE
    for g, win in enumerate(POOL_WINDOWS):
        cur = ubuf[g, POOL_PAD:, :]
        s_p = cur
        for k in range(1, win):
            s_p = s_p + ubuf[g, POOL_PAD - k:POOL_PAD - k + TILE, :]
        cnt = jnp.minimum(pos + 1, win).astype(F32)
        d = [s_p / cnt - cur]
        for s in seqs:
            base = s * POOL_PITCH + POOL_PAD
            cur_s = usub[g, base:base + DEC_SEQ, :]
            s_s = cur_s
            for k in range(1, win):
                s_s = s_s + usub[g, base - k:base - k + DEC_SEQ, :]
            d.append(s_s * (1.0 / win) - cur_s)
        d = jnp.concatenate(d, axis=0)
        y = jnp.dot(d.astype(BF16), wgrp_ref[g], preferred_element_type=F32) * scale_ref[:, _lanes(g)]
        mix_ref[:, _lanes(g)] = y.astype(BF16)

    for sl in range(CONV_WIDTH // LANES):
        taps = [[cbuf[sl, SUBLANES - k:SUBLANES - k + TILE, :]] for k in (2, 1, 0)]
        for s in seqs:
            base = s * CONV_PITCH + SUBLANES
            for t, k in zip(taps, (2, 1, 0)):
                t.append(csub[sl, base - k:base - k + DEC_SEQ, :])
        yc = gb[:, _lanes(sl)] * _taps(cw_ref[:, _lanes(sl)], *[jnp.concatenate(t, axis=0) for t in taps])
        mix_ref[:, POOL_WIDTH + sl * LANES:POOL_WIDTH + (sl + 1) * LANES] = yc.astype(BF16)

    x1 = x + jnp.dot(mix_ref[...], wout_ref[...], preferred_element_type=F32)
    h_ref[...] = _rmsnorm(x1, g2_ref[...]).astype(BF16)

    def put_up(c, up):
        buf, sub = fbuf.at[c % FF_BUFFERS], fsub.at[c % FF_BUFFERS]
        for sl in range(CHUNK_SLABS):
            cols = _ff_slab_cols(c, sl)
            buf[sl, 0:SUBLANES, :] = hf_ref[:, cols]
            buf[sl, SUBLANES:, :] = up[:TILE, _lanes(sl)]
            hf_ref[:, cols] = up[TILE - SUBLANES:TILE, _lanes(sl)]
            for s in seqs:
                base = s * CONV_PITCH + SUBLANES
                sub[sl, base - (CONV_K - 1):base, :] = sf_ref[s, :, cols]
                sub[sl, base:base + DEC_SEQ, :] = up[seq_rows(s), _lanes(sl)]
                fos_ref[s, :, cols] = up[TILE + (s + 1) * DEC_SEQ - (CONV_K - 1):TILE + (s + 1) * DEC_SEQ, _lanes(sl)]

    def conv_up(c, cw):
        buf, sub = fbuf.at[c % FF_BUFFERS], fsub.at[c % FF_BUFFERS]
        outs = []
        for sl in range(CHUNK_SLABS):
            taps = [[buf[sl, SUBLANES - k:SUBLANES - k + TILE, :]] for k in (2, 1, 0)]
            for s in seqs:
                base = s * CONV_PITCH + SUBLANES
                for t, k in zip(taps, (2, 1, 0)):
                    t.append(sub[sl, base - k:base - k + DEC_SEQ, :])
            outs.append(_taps(cw[:, _lanes(sl)], *[jnp.concatenate(t, axis=0) for t in taps]))
        return jnp.concatenate(outs, axis=1)

    x2 = x1 + _conv_ffn(slot_ref, h_ref, act_refs, wup_ref, fcw_ref, wdown_ref, put_up, conv_up)
    fo_ref[...] = hf_ref[SUBLANES - (CONV_K - 1):, :]
    y = _rmsnorm(x2, gf_ref[...])
    y_ref[...] = y[:TILE]
    ys_ref[...] = y[TILE:].reshape(RIDE_SEQ, DEC_SEQ, D_MODEL)


def _whole(a):
    nd = a.ndim
    return pl.BlockSpec(a.shape, lambda *_: (0,) * nd, pipeline_mode=pl.Buffered(1))


def kernel(x_prompt, x_sample, state_pool, state_conv, state_ffn, norm1_g, w_in, w_pool_grp, pool_scale,
           conv_w, w_out, norm2_g, w_up, ffn_conv_w, w_down, final_g):
    depth = w_in.shape[0]
    assert depth == 1
    batch, seq, _ = x_prompt.shape
    dec_batch, dec_seq, _ = x_sample.shape
    n_tiles = seq // TILE
    steps = batch * n_tiles
    assert seq % TILE == 0 and dec_seq == DEC_SEQ and dec_batch == RIDE_SEQ * steps and SUBLANES % RIDE_SEQ == 0
    assert all(w.shape[1] % STAGE_ROWS == 0 and w.shape[2] % STAGE_COLS == 0 for w in (w_in, w_out, w_up, w_down))

    big_f32 = (w_in[0], w_out[0], w_up[0], w_down[0])
    g1, g2, gf = norm1_g[0].reshape(1, D_MODEL), norm2_g[0].reshape(1, D_MODEL), final_g.reshape(1, D_MODEL)
    wgrp, scale = w_pool_grp[0].astype(BF16), pool_scale[0].reshape(1, POOL_WIDTH)
    cw, fcw = conv_w[0], ffn_conv_w[0]
    sp = jnp.transpose(state_pool[0], (1, 0, 2))
    sc, sf = state_conv[0], state_ffn[0]
    slots = jnp.zeros((2,), jnp.int32)
    slot_spec = pl.BlockSpec(memory_space=pltpu.SMEM)
    hbm_spec = pl.BlockSpec(memory_space=pl.ANY)

    def tile_of(i):
        t = jnp.maximum(i - 1, 0)
        return t // n_tiles, t % n_tiles

    def ride_of(i):
        return jnp.maximum(i - 1, 0)

    x_spec = pl.BlockSpec((None, TILE, D_MODEL), lambda i: (*tile_of(i), 0))
    xs_spec = pl.BlockSpec((RIDE_SEQ, DEC_SEQ, D_MODEL), lambda i: (ride_of(i), 0, 0))
    sp_spec = pl.BlockSpec((POOL_HIST, SUBLANES, POOL_WIDTH), lambda i: (0, ride_of(i) // (SUBLANES // RIDE_SEQ), 0))
    sc_spec = pl.BlockSpec((RIDE_SEQ, CONV_K - 1, CONV_WIDTH), lambda i: (ride_of(i), 0, 0))
    sf_spec = pl.BlockSpec((RIDE_SEQ, CONV_K - 1, 2 * D_FF), lambda i: (ride_of(i), 0, 0))
    state_rows = (POOL_HIST, CONV_K - 1, CONV_K - 1)
    state_widths = (POOL_WIDTH, CONV_WIDTH, 2 * D_FF)

    operands = [(slots, slot_spec), (x_prompt, x_spec), (x_sample, xs_spec), (sp, sp_spec), (sc, sc_spec),
                (sf, sf_spec), (g1, _whole(g1)), (big_f32[0], hbm_spec), (wgrp, _whole(wgrp)),
                (scale, _whole(scale)), (cw, _whole(cw)), (big_f32[1], hbm_spec), (g2, _whole(g2)),
                (big_f32[2], hbm_spec), (fcw, _whole(fcw)), (big_f32[3], hbm_spec), (gf, _whole(gf))]
    yp, pp, cp, fp, ys, ps, cs, fs = pl.pallas_call(
        functools.partial(_layer_kernel, n_tiles),
        grid=(1 + steps,),
        in_specs=[spec for _, spec in operands],
        out_specs=[x_spec] + [
            pl.BlockSpec((None, r, w), lambda i: (tile_of(i)[0], 0, 0)) for r, w in zip(state_rows, state_widths)
        ] + [xs_spec, sp_spec, sc_spec, sf_spec],
        out_shape=[jax.ShapeDtypeStruct((batch, seq, D_MODEL), F32)] + [
            jax.ShapeDtypeStruct((batch, r, w), F32) for r, w in zip(state_rows, state_widths)
        ] + [jax.ShapeDtypeStruct(a.shape, F32) for a in (x_sample, sp, sc, sf)],
        scratch_shapes=[pltpu.VMEM(w.shape, BF16) for w in big_f32] + [
            pltpu.VMEM((STAGE_DEPTH, STAGE_ROWS, STAGE_COLS), F32),
            pltpu.SemaphoreType.DMA((STAGE_DEPTH,)),
            pltpu.VMEM((POOL_PAD, POOL_WIDTH), F32),
            pltpu.VMEM((SUBLANES, CONV_WIDTH), F32),
            pltpu.VMEM((SUBLANES, 2 * D_FF), F32),
            pltpu.VMEM((ROWS, D_MODEL), BF16),
            pltpu.VMEM((ROWS, D_MODEL), BF16),
            pltpu.VMEM((POOL_WIDTH // LANES, POOL_PAD + TILE, LANES), F32),
            pltpu.VMEM((CONV_WIDTH // LANES, SUBLANES + TILE, LANES), F32),
            pltpu.VMEM((FF_BUFFERS, CHUNK_SLABS, SUBLANES + TILE, LANES), F32),
            pltpu.VMEM((POOL_WIDTH // LANES, RIDE_SEQ * POOL_PITCH, LANES), F32),
            pltpu.VMEM((CONV_WIDTH // LANES, RIDE_SEQ * CONV_PITCH, LANES), F32),
            pltpu.VMEM((FF_BUFFERS, CHUNK_SLABS, RIDE_SEQ * CONV_PITCH, LANES), F32),
        ] + [pltpu.VMEM((1, ROWS, FF_CHUNK), BF16)] * N_CHUNKS,
        compiler_params=pltpu.CompilerParams(
            dimension_semantics=("arbitrary",), vmem_limit_bytes=VMEM_LIMIT_BYTES),
        name="decoder_layer",
    )(*[a for a, _ in operands])

    return (yp, ys, pp[None], cp[None], fp[None], jnp.transpose(ps, (1, 0, 2))[None], cs[None], fs[None])
```

```python
import functools

import jax
import jax.numpy as jnp
from jax import lax
from jax.experimental import pallas as pl
from jax.experimental.pallas import tpu as pltpu

D_MODEL = 1024
POOL_WIDTH = 512
CONV_WIDTH = 512
POOL_WINDOWS = (2, 4, 8, 16)
GROUP_DIM = 128
POOL_HIST = 15
CONV_K = 3
D_FF = 2816
IN_WIDTH = POOL_WIDTH + 3 * CONV_WIDTH
EPS = 1e-6

LANES = 128
SUBLANES = 8
TILE = 512
DEC_SEQ = 8
RIDE_SEQ = 4
RIDE_ROWS = RIDE_SEQ * DEC_SEQ
ROWS = TILE + RIDE_ROWS
FF_CHUNK = 256
N_CHUNKS = D_FF // FF_CHUNK
CHUNK_SLABS = 2 * FF_CHUNK // LANES
FF_BUFFERS = 3
POOL_PAD = 16
POOL_PITCH = POOL_PAD + DEC_SEQ
CONV_PITCH = SUBLANES + DEC_SEQ
STAGE_ROWS, STAGE_COLS = 256, 512
STAGE_DEPTH = 8
VMEM_LIMIT_BYTES = 58 * 1024 * 1024

F32 = jnp.float32
BF16 = jnp.bfloat16


def _rmsnorm(x, g):
    ms = jnp.sum(x * x, axis=-1, keepdims=True) * (1.0 / D_MODEL)
    return (x * lax.rsqrt(ms + EPS)) * g


def _silu(x):
    return x * (1.0 / (1.0 + jnp.exp(-x)))


def _lanes(i):
    return slice(i * LANES, (i + 1) * LANES)


def _ff_cols(c):
    g0 = c * FF_CHUNK
    v0 = D_FF + c * FF_CHUNK
    return slice(g0, g0 + FF_CHUNK), slice(v0, v0 + FF_CHUNK)


def _ff_slab_cols(c, s):
    gcols, vcols = _ff_cols(c)
    half = FF_CHUNK // LANES
    base = gcols.start if s < half else vcols.start
    off = (s % half) * LANES
    return slice(base + off, base + off + LANES)


def _taps(cw, s2, s1, s0):
    return cw[0:1, :] * s2 + cw[1:2, :] * s1 + cw[2:3, :] * s0


def _proj(h_ref, w_ref, cols):
    return jnp.dot(h_ref[...], w_ref[:, cols], preferred_element_type=F32)


def _conv_ffn(slot_ref, h_ref, act_refs, wup_ref, fcw_ref, wdown_ref, put_up, conv_up):
    def up_proj(c):
        gcols, vcols = _ff_cols(c)
        w_c = jnp.concatenate([wup_ref[:, gcols], wup_ref[:, vcols]], axis=1)
        return jnp.dot(h_ref[...], w_c, preferred_element_type=F32)

    put_up(0, up_proj(0))
    for c in range(N_CHUNKS):
        if c + 1 < N_CHUNKS:
            put_up(c + 1, up_proj(c + 1))
        gcols, vcols = _ff_cols(c)
        cw = jnp.concatenate([fcw_ref[:, gcols], fcw_ref[:, vcols]], axis=1)
        upc = conv_up(c, cw)
        act = _silu(upc[:, :FF_CHUNK]) * upc[:, FF_CHUNK:]
        act_refs[c][slot_ref[0]] = act.astype(BF16)

    down = None
    for c in range(N_CHUNKS):
        gcols, _ = _ff_cols(c)
        part = jnp.dot(act_refs[c][slot_ref[1]], wdown_ref[gcols, :], preferred_element_type=F32)
        down = part if down is None else down + part
    return down


def _cast_weights(srcs, dsts, stage, in_sems):
    depth = stage.shape[0]
    pieces = [(k, r, c) for k, src in enumerate(srcs)
              for r in range(0, src.shape[0], STAGE_ROWS) for c in range(0, src.shape[1], STAGE_COLS)]

    def fetch(p):
        k, r, c = pieces[p]
        window = srcs[k].at[r:r + STAGE_ROWS, c:c + STAGE_COLS]
        return pltpu.make_async_copy(window, stage.at[p % depth], in_sems.at[p % depth])

    for p in range(depth):
        fetch(p).start()
    for p, (k, r, c) in enumerate(pieces):
        fetch(p).wait()
        dsts[k][r:r + STAGE_ROWS, c:c + STAGE_COLS] = stage[p % depth].astype(BF16)
        if p + depth < len(pieces):
            fetch(p + depth).start()


def _layer_kernel(n_tiles, slot_ref, x_ref, xs_ref, sp_ref, sc_ref, sf_ref, g1_ref, win_hbm, wgrp_ref, scale_ref,
                  cw_ref, wout_hbm, g2_ref, wup_hbm, fcw_ref, wdown_hbm, gf_ref,
                  y_ref, po_ref, co_ref, fo_ref, ys_ref, pos_ref, cos_ref, fos_ref,
                  win_ref, wout_ref, wup_ref, wdown_ref, stage_ref, in_sems, wo32_ref, fold_sem,
                  hu_ref, hc_ref, hf_ref, h_ref, mix_ref, ubuf, cbuf, fbuf, usub, csub, fsub, *act_refs):
    i = pl.program_id(0)

    @pl.when(i == 0)
    def _():
        pool_rows = pltpu.make_async_copy(wout_hbm.at[0:POOL_WIDTH, :], wo32_ref, fold_sem)
        pool_rows.start()
        _cast_weights((win_hbm, wout_hbm, wup_hbm, wdown_hbm), (win_ref, wout_ref, wup_ref, wdown_ref),
                      stage_ref, in_sems)
        pool_rows.wait()
        for g in range(len(POOL_WINDOWS)):
            folded = jnp.dot(wgrp_ref[g] * scale_ref[:, _lanes(g)], wo32_ref[_lanes(g), :],
                             preferred_element_type=F32, precision=lax.Precision.HIGHEST)
            wout_ref[_lanes(g), :] = folded.astype(BF16)

    @pl.when(i > 0)
    def _():
        part = lax.rem(i - 1, SUBLANES // RIDE_SEQ)
        for p in range(SUBLANES // RIDE_SEQ):
            @pl.when(part == p)
            def _(p=p):
                for g in range(len(POOL_WINDOWS)):
                    for s in range(RIDE_SEQ):
                        usub[g, s * POOL_PITCH + 1:s * POOL_PITCH + POOL_PAD, :] = sp_ref[:, p * RIDE_SEQ + s, _lanes(g)]

        _tile(lax.rem(i - 1, n_tiles), slot_ref, x_ref, xs_ref, sc_ref, sf_ref, g1_ref, win_ref,
              cw_ref, wout_ref, g2_ref, wup_ref, fcw_ref, wdown_ref, gf_ref, y_ref, po_ref, co_ref, fo_ref,
              ys_ref, cos_ref, fos_ref, hu_ref, hc_ref, hf_ref, h_ref, mix_ref, ubuf, cbuf, fbuf, usub, csub, fsub,
              act_refs)

        for p in range(SUBLANES // RIDE_SEQ):
            @pl.when(part == p)
            def _(p=p):
                for g in range(len(POOL_WINDOWS)):
                    for s in range(RIDE_SEQ):
                        lo = s * POOL_PITCH + POOL_PITCH - POOL_HIST
                        pos_ref[:, p * RIDE_SEQ + s, _lanes(g)] = usub[g, lo:lo + POOL_HIST, :]


def _tile(j, slot_ref, x_ref, xs_ref, sc_ref, sf_ref, g1_ref, win_ref, cw_ref, wout_ref, g2_ref,
          wup_ref, fcw_ref, wdown_ref, gf_ref, y_ref, po_ref, co_ref, fo_ref, ys_ref, cos_ref, fos_ref,
          hu_ref, hc_ref, hf_ref, h_ref, mix_ref, ubuf, cbuf, fbuf, usub, csub, fsub, act_refs):
    seqs = range(RIDE_SEQ)

    def seq_rows(s):
        return slice(TILE + s * DEC_SEQ, TILE + (s + 1) * DEC_SEQ)

    @pl.when(j == 0)
    def _():
        hu_ref[...] = jnp.zeros_like(hu_ref)
        hc_ref[...] = jnp.zeros_like(hc_ref)
        hf_ref[...] = jnp.zeros_like(hf_ref)

    x = jnp.concatenate([x_ref[...], xs_ref[...].reshape(RIDE_ROWS, D_MODEL)], axis=0)
    h_ref[...] = _rmsnorm(x, g1_ref[...]).astype(BF16)

    u = _proj(h_ref, win_ref, slice(0, POOL_WIDTH))
    for g in range(len(POOL_WINDOWS)):
        ubuf[g, 0:POOL_PAD, :] = hu_ref[:, _lanes(g)]
        ubuf[g, POOL_PAD:, :] = u[:TILE, _lanes(g)]
        hu_ref[:, _lanes(g)] = u[TILE - POOL_PAD:TILE, _lanes(g)]
        for s in seqs:
            usub[g, s * POOL_PITCH + POOL_PAD:(s + 1) * POOL_PITCH, :] = u[seq_rows(s), _lanes(g)]
    po_ref[...] = hu_ref[POOL_PAD - POOL_HIST:, :]

    gc = _proj(h_ref, win_ref, slice(POOL_WIDTH + CONV_WIDTH, POOL_WIDTH + 2 * CONV_WIDTH))
    hv = _proj(h_ref, win_ref, slice(POOL_WIDTH + 2 * CONV_WIDTH, IN_WIDTH))
    v = gc * hv
    for sl in range(CONV_WIDTH // LANES):
        cbuf[sl, 0:SUBLANES, :] = hc_ref[:, _lanes(sl)]
        cbuf[sl, SUBLANES:, :] = v[:TILE, _lanes(sl)]
        hc_ref[:, _lanes(sl)] = v[TILE - SUBLANES:TILE, _lanes(sl)]
        for s in seqs:
            base = s * CONV_PITCH + SUBLANES
            csub[sl, base - (CONV_K - 1):base, :] = sc_ref[s, :, _lanes(sl)]
            csub[sl, base:base + DEC_SEQ, :] = v[seq_rows(s), _lanes(sl)]
            cos_ref[s, :, _lanes(sl)] = v[TILE + (s + 1) * DEC_SEQ - (CONV_K - 1):TILE + (s + 1) * DEC_SEQ, _lanes(sl)]
    co_ref[...] = hc_ref[SUBLANES - (CONV_K - 1):, :]
    gb = _proj(h_ref, win_ref, slice(POOL_WIDTH, POOL_WIDTH + CONV_WIDTH))

    pos = lax.broadcasted_iota(jnp.int32, (TILE, GROUP_DIM), 0) + j * TILE
    for g, win in enumerate(POOL_WINDOWS):
        cur = ubuf[g, POOL_PAD:, :]
        s_p = cur
        for k in range(1, win):
            s_p = s_p + ubuf[g, POOL_PAD - k:POOL_PAD - k + TILE, :]
        cnt = jnp.minimum(pos + 1, win).astype(F32)
        d = [s_p / cnt - cur]
        for s in seqs:
            base = s * POOL_PITCH + POOL_PAD
            cur_s = usub[g, base:base + DEC_SEQ, :]
            s_s = cur_s
            for k in range(1, win):
                s_s = s_s + usub[g, base - k:base - k + DEC_SEQ, :]
            d.append(s_s * (1.0 / win) - cur_s)
        mix_ref[:, _lanes(g)] = jnp.concatenate(d, axis=0).astype(BF16)

    for sl in range(CONV_WIDTH // LANES):
        taps = [[cbuf[sl, SUBLANES - k:SUBLANES - k + TILE, :]] for k in (2, 1, 0)]
        for s in seqs:
            base = s * CONV_PITCH + SUBLANES
            for t, k in zip(taps, (2, 1, 0)):
                t.append(csub[sl, base - k:base - k + DEC_SEQ, :])
        yc = gb[:, _lanes(sl)] * _taps(cw_ref[:, _lanes(sl)], *[jnp.concatenate(t, axis=0) for t in taps])
        mix_ref[:, POOL_WIDTH + sl * LANES:POOL_WIDTH + (sl + 1) * LANES] = yc.astype(BF16)

    x1 = x + jnp.dot(mix_ref[...], wout_ref[...], preferred_element_type=F32)
    h_ref[...] = _rmsnorm(x1, g2_ref[...]).astype(BF16)

    def put_up(c, up):
        buf, sub = fbuf.at[c % FF_BUFFERS], fsub.at[c % FF_BUFFERS]
        for sl in range(CHUNK_SLABS):
            cols = _ff_slab_cols(c, sl)
            buf[sl, 0:SUBLANES, :] = hf_ref[:, cols]
            buf[sl, SUBLANES:, :] = up[:TILE, _lanes(sl)]
            hf_ref[:, cols] = up[TILE - SUBLANES:TILE, _lanes(sl)]
            for s in seqs:
                base = s * CONV_PITCH + SUBLANES
                sub[sl, base - (CONV_K - 1):base, :] = sf_ref[s, :, cols]
                sub[sl, base:base + DEC_SEQ, :] = up[seq_rows(s), _lanes(sl)]
                fos_ref[s, :, cols] = up[TILE + (s + 1) * DEC_SEQ - (CONV_K - 1):TILE + (s + 1) * DEC_SEQ, _lanes(sl)]

    def conv_up(c, cw):
        buf, sub = fbuf.at[c % FF_BUFFERS], fsub.at[c % FF_BUFFERS]
        outs = []
        for sl in range(CHUNK_SLABS):
            taps = [[buf[sl, SUBLANES - k:SUBLANES - k + TILE, :]] for k in (2, 1, 0)]
            for s in seqs:
                base = s * CONV_PITCH + SUBLANES
                for t, k in zip(taps, (2, 1, 0)):
                    t.append(sub[sl, base - k:base - k + DEC_SEQ, :])
            outs.append(_taps(cw[:, _lanes(sl)], *[jnp.concatenate(t, axis=0) for t in taps]))
        return jnp.concatenate(outs, axis=1)

    x2 = x1 + _conv_ffn(slot_ref, h_ref, act_refs, wup_ref, fcw_ref, wdown_ref, put_up, conv_up)
    fo_ref[...] = hf_ref[SUBLANES - (CONV_K - 1):, :]
    y = _rmsnorm(x2, gf_ref[...])
    y_ref[...] = y[:TILE]
    ys_ref[...] = y[TILE:].reshape(RIDE_SEQ, DEC_SEQ, D_MODEL)


def _whole(a):
    nd = a.ndim
    return pl.BlockSpec(a.shape, lambda *_: (0,) * nd, pipeline_mode=pl.Buffered(1))


def kernel(x_prompt, x_sample, state_pool, state_conv, state_ffn, norm1_g, w_in, w_pool_grp, pool_scale,
           conv_w, w_out, norm2_g, w_up, ffn_conv_w, w_down, final_g):
    depth = w_in.shape[0]
    assert depth == 1
    batch, seq, _ = x_prompt.shape
    dec_batch, dec_seq, _ = x_sample.shape
    n_tiles = seq // TILE
    steps = batch * n_tiles
    assert seq % TILE == 0 and dec_seq == DEC_SEQ and dec_batch == RIDE_SEQ * steps and SUBLANES % RIDE_SEQ == 0
    assert all(w.shape[1] % STAGE_ROWS == 0 and w.shape[2] % STAGE_COLS == 0 for w in (w_in, w_out, w_up, w_down))

    big_f32 = (w_in[0], w_out[0], w_up[0], w_down[0])
    g1, g2, gf = norm1_g[0].reshape(1, D_MODEL), norm2_g[0].reshape(1, D_MODEL), final_g.reshape(1, D_MODEL)
    wgrp, scale = w_pool_grp[0], pool_scale[0].reshape(1, POOL_WIDTH)
    cw, fcw = conv_w[0], ffn_conv_w[0]
    sp = jnp.transpose(state_pool[0], (1, 0, 2))
    sc, sf = state_conv[0], state_ffn[0]
    slots = jnp.zeros((2,), jnp.int32)
    slot_spec = pl.BlockSpec(memory_space=pltpu.SMEM)
    hbm_spec = pl.BlockSpec(memory_space=pl.ANY)

    def tile_of(i):
        t = jnp.maximum(i - 1, 0)
        return t // n_tiles, t % n_tiles

    def ride_of(i):
        return jnp.maximum(i - 1, 0)

    x_spec = pl.BlockSpec((None, TILE, D_MODEL), lambda i: (*tile_of(i), 0))
    xs_spec = pl.BlockSpec((RIDE_SEQ, DEC_SEQ, D_MODEL), lambda i: (ride_of(i), 0, 0))
    sp_spec = pl.BlockSpec((POOL_HIST, SUBLANES, POOL_WIDTH), lambda i: (0, ride_of(i) // (SUBLANES // RIDE_SEQ), 0))
    sc_spec = pl.BlockSpec((RIDE_SEQ, CONV_K - 1, CONV_WIDTH), lambda i: (ride_of(i), 0, 0))
    sf_spec = pl.BlockSpec((RIDE_SEQ, CONV_K - 1, 2 * D_FF), lambda i: (ride_of(i), 0, 0))
    state_rows = (POOL_HIST, CONV_K - 1, CONV_K - 1)
    state_widths = (POOL_WIDTH, CONV_WIDTH, 2 * D_FF)

    operands = [(slots, slot_spec), (x_prompt, x_spec), (x_sample, xs_spec), (sp, sp_spec), (sc, sc_spec),
                (sf, sf_spec), (g1, _whole(g1)), (big_f32[0], hbm_spec), (wgrp, _whole(wgrp)),
                (scale, _whole(scale)), (cw, _whole(cw)), (big_f32[1], hbm_spec), (g2, _whole(g2)),
                (big_f32[2], hbm_spec), (fcw, _whole(fcw)), (big_f32[3], hbm_spec), (gf, _whole(gf))]
    yp, pp, cp, fp, ys, ps, cs, fs = pl.pallas_call(
        functools.partial(_layer_kernel, n_tiles),
        grid=(1 + steps,),
        in_specs=[spec for _, spec in operands],
        out_specs=[x_spec] + [
            pl.BlockSpec((None, r, w), lambda i: (tile_of(i)[0], 0, 0)) for r, w in zip(state_rows, state_widths)
        ] + [xs_spec, sp_spec, sc_spec, sf_spec],
        out_shape=[jax.ShapeDtypeStruct((batch, seq, D_MODEL), F32)] + [
            jax.ShapeDtypeStruct((batch, r, w), F32) for r, w in zip(state_rows, state_widths)
        ] + [jax.ShapeDtypeStruct(a.shape, F32) for a in (x_sample, sp, sc, sf)],
        scratch_shapes=[pltpu.VMEM(w.shape, BF16) for w in big_f32] + [
            pltpu.VMEM((STAGE_DEPTH, STAGE_ROWS, STAGE_COLS), F32),
            pltpu.SemaphoreType.DMA((STAGE_DEPTH,)),
            pltpu.VMEM((POOL_WIDTH, D_MODEL), F32),
            pltpu.SemaphoreType.DMA(()),
            pltpu.VMEM((POOL_PAD, POOL_WIDTH), F32),
            pltpu.VMEM((SUBLANES, CONV_WIDTH), F32),
            pltpu.VMEM((SUBLANES, 2 * D_FF), F32),
            pltpu.VMEM((ROWS, D_MODEL), BF16),
            pltpu.VMEM((ROWS, D_MODEL), BF16),
            pltpu.VMEM((POOL_WIDTH // LANES, POOL_PAD + TILE, LANES), F32),
            pltpu.VMEM((CONV_WIDTH // LANES, SUBLANES + TILE, LANES), F32),
            pltpu.VMEM((FF_BUFFERS, CHUNK_SLABS, SUBLANES + TILE, LANES), F32),
            pltpu.VMEM((POOL_WIDTH // LANES, RIDE_SEQ * POOL_PITCH, LANES), F32),
            pltpu.VMEM((CONV_WIDTH // LANES, RIDE_SEQ * CONV_PITCH, LANES), F32),
            pltpu.VMEM((FF_BUFFERS, CHUNK_SLABS, RIDE_SEQ * CONV_PITCH, LANES), F32),
        ] + [pltpu.VMEM((1, ROWS, FF_CHUNK), BF16)] * N_CHUNKS,
        compiler_params=pltpu.CompilerParams(
            dimension_semantics=("arbitrary",), vmem_limit_bytes=VMEM_LIMIT_BYTES),
        name="decoder_layer",
    )(*[a for a, _ in operands])

    return (yp, ys, pp[None], cp[None], fp[None], jnp.transpose(ps, (1, 0, 2))[None], cs[None], fs[None])
```

```python
import functools

import jax
import jax.numpy as jnp
import numpy as np
from jax import lax
from jax.experimental import pallas as pl
from jax.experimental.pallas import tpu as pltpu

D_MODEL = 1024
POOL_WIDTH = 512
CONV_WIDTH = 512
POOL_WINDOWS = (2, 4, 8, 16)
GROUP_DIM = 128
POOL_HIST = 15
CONV_K = 3
D_FF = 2816
IN_WIDTH = POOL_WIDTH + 3 * CONV_WIDTH
EPS = 1e-6

LANES = 128
SUBLANES = 8
TILE = 512
DEC_SEQ = 8
RIDE_SEQ = 4
RIDE_ROWS = RIDE_SEQ * DEC_SEQ
ROWS = TILE + RIDE_ROWS
FF_CHUNK = 256
N_CHUNKS = D_FF // FF_CHUNK
CHUNK_SLABS = 2 * FF_CHUNK // LANES
FF_BUFFERS = 3
POOL_PAD = 16
POOL_PITCH = POOL_PAD + DEC_SEQ
CONV_PITCH = SUBLANES + DEC_SEQ
STAGE_ROWS, STAGE_COLS = 256, 512
STAGE_DEPTH = 8
VMEM_LIMIT_BYTES = 58 * 1024 * 1024

F32 = jnp.float32
BF16 = jnp.bfloat16


def _rmsnorm(x, g):
    ms = jnp.sum(x * x, axis=-1, keepdims=True) * (1.0 / D_MODEL)
    return (x * lax.rsqrt(ms + EPS)) * g


def _silu(x):
    return x * (1.0 / (1.0 + jnp.exp(-x)))


def _lanes(i):
    return slice(i * LANES, (i + 1) * LANES)


def _ff_cols(c):
    g0 = c * FF_CHUNK
    v0 = D_FF + c * FF_CHUNK
    return slice(g0, g0 + FF_CHUNK), slice(v0, v0 + FF_CHUNK)


def _ff_slab_cols(c, s):
    gcols, vcols = _ff_cols(c)
    half = FF_CHUNK // LANES
    base = gcols.start if s < half else vcols.start
    off = (s % half) * LANES
    return slice(base + off, base + off + LANES)


def _taps(cw, s2, s1, s0):
    return cw[0] * s2 + cw[1] * s1 + cw[2] * s0


def _proj(h_ref, w_ref, cols):
    return jnp.dot(h_ref[...], w_ref[:, cols], preferred_element_type=F32)


def _conv_ffn(slot_ref, h_ref, act_refs, wup_ref, fcw_ref, wdown_ref, put_up, conv_up):
    def up_proj(c):
        gcols, vcols = _ff_cols(c)
        w_c = jnp.concatenate([wup_ref[:, gcols], wup_ref[:, vcols]], axis=1)
        return jnp.dot(h_ref[...], w_c, preferred_element_type=F32)

    put_up(0, up_proj(0))
    for c in range(N_CHUNKS):
        if c + 1 < N_CHUNKS:
            put_up(c + 1, up_proj(c + 1))
        gcols, vcols = _ff_cols(c)
        cw = jnp.concatenate([fcw_ref[:, :, gcols], fcw_ref[:, :, vcols]], axis=2)
        upc = conv_up(c, cw)
        act = _silu(upc[:, :FF_CHUNK]) * upc[:, FF_CHUNK:]
        act_refs[c][slot_ref[0]] = act.astype(BF16)

    down = None
    for c in range(N_CHUNKS):
        gcols, _ = _ff_cols(c)
        part = jnp.dot(act_refs[c][slot_ref[1]], wdown_ref[gcols, :], preferred_element_type=F32)
        down = part if down is None else down + part
    return down


def _cast_weights(srcs, dsts, stage, in_sems):
    depth = stage.shape[0]
    pieces = [(k, r, c) for k, src in enumerate(srcs)
              for r in range(0, src.shape[0], STAGE_ROWS) for c in range(0, src.shape[1], STAGE_COLS)]

    def fetch(p):
        k, r, c = pieces[p]
        window = srcs[k].at[r:r + STAGE_ROWS, c:c + STAGE_COLS]
        return pltpu.make_async_copy(window, stage.at[p % depth], in_sems.at[p % depth])

    for p in range(depth):
        fetch(p).start()
    for p, (k, r, c) in enumerate(pieces):
        fetch(p).wait()
        dsts[k][r:r + STAGE_ROWS, c:c + STAGE_COLS] = stage[p % depth].astype(BF16)
        if p + depth < len(pieces):
            fetch(p + depth).start()


def _layer_kernel(n_tiles, slot_ref, x_ref, xs_ref, sp_ref, sc_ref, sf_ref, g1_ref, win_hbm, wgrp_ref, scale_ref,
                  cw_ref, wout_hbm, g2_ref, wup_hbm, fcw_ref, wdown_hbm, gf_ref,
                  y_ref, po_ref, co_ref, fo_ref, ys_ref, pos_ref, cos_ref, fos_ref,
                  win_ref, wout_ref, wup_ref, wdown_ref, stage_ref, in_sems, wo32_ref, fold_sem,
                  hu_ref, hc_ref, hf_ref, h_ref, mix_ref, ubuf, cbuf, fbuf, usub, csub, fsub, *act_refs):
    i = pl.program_id(0)

    @pl.when(i == 0)
    def _():
        pool_rows = pltpu.make_async_copy(wout_hbm.at[0:POOL_WIDTH, :], wo32_ref, fold_sem)
        pool_rows.start()
        _cast_weights((win_hbm, wout_hbm, wup_hbm, wdown_hbm), (win_ref, wout_ref, wup_ref, wdown_ref),
                      stage_ref, in_sems)
        pool_rows.wait()
        for g in range(len(POOL_WINDOWS)):
            folded = jnp.dot(wgrp_ref[g] * scale_ref[:, _lanes(g)], wo32_ref[_lanes(g), :],
                             preferred_element_type=F32, precision=lax.Precision.HIGHEST)
            wout_ref[_lanes(g), :] = folded.astype(BF16)

    @pl.when(i > 0)
    def _():
        part = lax.rem(i - 1, SUBLANES // RIDE_SEQ)
        for p in range(SUBLANES // RIDE_SEQ):
            @pl.when(part == p)
            def _(p=p):
                for g in range(len(POOL_WINDOWS)):
                    for s in range(RIDE_SEQ):
                        usub[g, s * POOL_PITCH + 1:s * POOL_PITCH + POOL_PAD, :] = sp_ref[:, p * RIDE_SEQ + s, _lanes(g)]

        _tile(lax.rem(i - 1, n_tiles), slot_ref, x_ref, xs_ref, sc_ref, sf_ref, g1_ref, win_ref,
              cw_ref, wout_ref, g2_ref, wup_ref, fcw_ref, wdown_ref, gf_ref, y_ref, po_ref, co_ref, fo_ref,
              ys_ref, cos_ref, fos_ref, hu_ref, hc_ref, hf_ref, h_ref, mix_ref, ubuf, cbuf, fbuf, usub, csub, fsub,
              act_refs)

        for p in range(SUBLANES // RIDE_SEQ):
            @pl.when(part == p)
            def _(p=p):
                for g in range(len(POOL_WINDOWS)):
                    for s in range(RIDE_SEQ):
                        lo = s * POOL_PITCH + POOL_PITCH - POOL_HIST
                        pos_ref[:, p * RIDE_SEQ + s, _lanes(g)] = usub[g, lo:lo + POOL_HIST, :]


def _tile(j, slot_ref, x_ref, xs_ref, sc_ref, sf_ref, g1_ref, win_ref, cw_ref, wout_ref, g2_ref,
          wup_ref, fcw_ref, wdown_ref, gf_ref, y_ref, po_ref, co_ref, fo_ref, ys_ref, cos_ref, fos_ref,
          hu_ref, hc_ref, hf_ref, h_ref, mix_ref, ubuf, cbuf, fbuf, usub, csub, fsub, act_refs):
    seqs = range(RIDE_SEQ)

    def seq_rows(s):
        return slice(TILE + s * DEC_SEQ, TILE + (s + 1) * DEC_SEQ)

    @pl.when(j == 0)
    def _():
        hu_ref[...] = jnp.zeros_like(hu_ref)
        hc_ref[...] = jnp.zeros_like(hc_ref)
        hf_ref[...] = jnp.zeros_like(hf_ref)

    x = jnp.concatenate([x_ref[...], xs_ref[...].reshape(RIDE_ROWS, D_MODEL)], axis=0)
    h_ref[...] = _rmsnorm(x, g1_ref[...]).astype(BF16)

    u = _proj(h_ref, win_ref, slice(0, POOL_WIDTH))
    for g in range(len(POOL_WINDOWS)):
        ubuf[g, 0:POOL_PAD, :] = hu_ref[:, _lanes(g)]
        ubuf[g, POOL_PAD:, :] = u[:TILE, _lanes(g)]
        hu_ref[:, _lanes(g)] = u[TILE - POOL_PAD:TILE, _lanes(g)]
        for s in seqs:
            usub[g, s * POOL_PITCH + POOL_PAD:(s + 1) * POOL_PITCH, :] = u[seq_rows(s), _lanes(g)]
    po_ref[...] = hu_ref[POOL_PAD - POOL_HIST:, :]

    gc = _proj(h_ref, win_ref, slice(POOL_WIDTH + CONV_WIDTH, POOL_WIDTH + 2 * CONV_WIDTH))
    hv = _proj(h_ref, win_ref, slice(POOL_WIDTH + 2 * CONV_WIDTH, IN_WIDTH))
    v = gc * hv
    for sl in range(CONV_WIDTH // LANES):
        cbuf[sl, 0:SUBLANES, :] = hc_ref[:, _lanes(sl)]
        cbuf[sl, SUBLANES:, :] = v[:TILE, _lanes(sl)]
        hc_ref[:, _lanes(sl)] = v[TILE - SUBLANES:TILE, _lanes(sl)]
        for s in seqs:
            base = s * CONV_PITCH + SUBLANES
            csub[sl, base - (CONV_K - 1):base, :] = sc_ref[s, :, _lanes(sl)]
            csub[sl, base:base + DEC_SEQ, :] = v[seq_rows(s), _lanes(sl)]
            cos_ref[s, :, _lanes(sl)] = v[TILE + (s + 1) * DEC_SEQ - (CONV_K - 1):TILE + (s + 1) * DEC_SEQ, _lanes(sl)]
    co_ref[...] = hc_ref[SUBLANES - (CONV_K - 1):, :]
    gb = _proj(h_ref, win_ref, slice(POOL_WIDTH, POOL_WIDTH + CONV_WIDTH))

    pos = lax.broadcasted_iota(jnp.int32, (TILE, GROUP_DIM), 0) + j * TILE
    for g, win in enumerate(POOL_WINDOWS):
        cur = ubuf[g, POOL_PAD:, :]
        s_p = cur
        for k in range(1, win):
            s_p = s_p + ubuf[g, POOL_PAD - k:POOL_PAD - k + TILE, :]
        cnt = jnp.minimum(pos + 1, win).astype(F32)
        d = [s_p / cnt - cur]
        for s in seqs:
            base = s * POOL_PITCH + POOL_PAD
            cur_s = usub[g, base:base + DEC_SEQ, :]
            s_s = cur_s
            for k in range(1, win):
                s_s = s_s + usub[g, base - k:base - k + DEC_SEQ, :]
            d.append(s_s * (1.0 / win) - cur_s)
        mix_ref[:, _lanes(g)] = jnp.concatenate(d, axis=0).astype(BF16)

    for sl in range(CONV_WIDTH // LANES):
        taps = [[cbuf[sl, SUBLANES - k:SUBLANES - k + TILE, :]] for k in (2, 1, 0)]
        for s in seqs:
            base = s * CONV_PITCH + SUBLANES
            for t, k in zip(taps, (2, 1, 0)):
                t.append(csub[sl, base - k:base - k + DEC_SEQ, :])
        yc = gb[:, _lanes(sl)] * _taps(cw_ref[:, :, _lanes(sl)], *[jnp.concatenate(t, axis=0) for t in taps])
        mix_ref[:, POOL_WIDTH + sl * LANES:POOL_WIDTH + (sl + 1) * LANES] = yc.astype(BF16)

    x1 = x + jnp.dot(mix_ref[...], wout_ref[...], preferred_element_type=F32)
    h_ref[...] = _rmsnorm(x1, g2_ref[...]).astype(BF16)

    def put_up(c, up):
        buf, sub = fbuf.at[c % FF_BUFFERS], fsub.at[c % FF_BUFFERS]
        for sl in range(CHUNK_SLABS):
            cols = _ff_slab_cols(c, sl)
            buf[sl, 0:SUBLANES, :] = hf_ref[:, cols]
            buf[sl, SUBLANES:, :] = up[:TILE, _lanes(sl)]
            hf_ref[:, cols] = up[TILE - SUBLANES:TILE, _lanes(sl)]
            for s in seqs:
                base = s * CONV_PITCH + SUBLANES
                sub[sl, base - (CONV_K - 1):base, :] = sf_ref[s, :, cols]
                sub[sl, base:base + DEC_SEQ, :] = up[seq_rows(s), _lanes(sl)]
                fos_ref[s, :, cols] = up[TILE + (s + 1) * DEC_SEQ - (CONV_K - 1):TILE + (s + 1) * DEC_SEQ, _lanes(sl)]

    def conv_up(c, cw):
        buf, sub = fbuf.at[c % FF_BUFFERS], fsub.at[c % FF_BUFFERS]
        outs = []
        for sl in range(CHUNK_SLABS):
            taps = [[buf[sl, SUBLANES - k:SUBLANES - k + TILE, :]] for k in (2, 1, 0)]
            for s in seqs:
                base = s * CONV_PITCH + SUBLANES
                for t, k in zip(taps, (2, 1, 0)):
                    t.append(sub[sl, base - k:base - k + DEC_SEQ, :])
            outs.append(_taps(cw[:, :, _lanes(sl)], *[jnp.concatenate(t, axis=0) for t in taps]))
        return jnp.concatenate(outs, axis=1)

    x2 = x1 + _conv_ffn(slot_ref, h_ref, act_refs, wup_ref, fcw_ref, wdown_ref, put_up, conv_up)
    fo_ref[...] = hf_ref[SUBLANES - (CONV_K - 1):, :]
    y = _rmsnorm(x2, gf_ref[...])
    y_ref[...] = y[:TILE]
    ys_ref[...] = y[TILE:].reshape(RIDE_SEQ, DEC_SEQ, D_MODEL)


def _whole(a):
    nd = a.ndim
    return pl.BlockSpec(a.shape, lambda *_: (0,) * nd, pipeline_mode=pl.Buffered(1))


def kernel(x_prompt, x_sample, state_pool, state_conv, state_ffn, norm1_g, w_in, w_pool_grp, pool_scale,
           conv_w, w_out, norm2_g, w_up, ffn_conv_w, w_down, final_g):
    depth = w_in.shape[0]
    assert depth == 1
    batch, seq, _ = x_prompt.shape
    dec_batch, dec_seq, _ = x_sample.shape
    n_tiles = seq // TILE
    steps = batch * n_tiles
    assert seq % TILE == 0 and dec_seq == DEC_SEQ and dec_batch == RIDE_SEQ * steps and SUBLANES % RIDE_SEQ == 0
    assert all(w.shape[1] % STAGE_ROWS == 0 and w.shape[2] % STAGE_COLS == 0 for w in (w_in, w_out, w_up, w_down))

    big_f32 = (w_in[0], w_out[0], w_up[0], w_down[0])
    g1, g2, gf = norm1_g[0].reshape(1, D_MODEL), norm2_g[0].reshape(1, D_MODEL), final_g.reshape(1, D_MODEL)
    wgrp, scale = w_pool_grp[0], pool_scale[0].reshape(1, POOL_WIDTH)
    cw, fcw = jnp.transpose(conv_w, (1, 0, 2)), jnp.transpose(ffn_conv_w, (1, 0, 2))
    sp = jnp.transpose(state_pool[0], (1, 0, 2))
    sc, sf = state_conv[0], state_ffn[0]
    slots = np.zeros((2,), np.int32)
    slot_spec = pl.BlockSpec(memory_space=pltpu.SMEM)
    hbm_spec = pl.BlockSpec(memory_space=pl.ANY)

    def tile_of(i):
        t = jnp.maximum(i - 1, 0)
        return t // n_tiles, t % n_tiles

    def ride_of(i):
        return jnp.maximum(i - 1, 0)

    x_spec = pl.BlockSpec((None, TILE, D_MODEL), lambda i: (*tile_of(i), 0))
    xs_spec = pl.BlockSpec((RIDE_SEQ, DEC_SEQ, D_MODEL), lambda i: (ride_of(i), 0, 0))
    sp_spec = pl.BlockSpec((POOL_HIST, SUBLANES, POOL_WIDTH), lambda i: (0, ride_of(i) // (SUBLANES // RIDE_SEQ), 0))
    sc_spec = pl.BlockSpec((RIDE_SEQ, CONV_K - 1, CONV_WIDTH), lambda i: (ride_of(i), 0, 0))
    sf_spec = pl.BlockSpec((RIDE_SEQ, CONV_K - 1, 2 * D_FF), lambda i: (ride_of(i), 0, 0))
    state_rows = (POOL_HIST, CONV_K - 1, CONV_K - 1)
    state_widths = (POOL_WIDTH, CONV_WIDTH, 2 * D_FF)

    operands = [(slots, slot_spec), (x_prompt, x_spec), (x_sample, xs_spec), (sp, sp_spec), (sc, sc_spec),
                (sf, sf_spec), (g1, _whole(g1)), (big_f32[0], hbm_spec), (wgrp, _whole(wgrp)),
                (scale, _whole(scale)), (cw, _whole(cw)), (big_f32[1], hbm_spec), (g2, _whole(g2)),
                (big_f32[2], hbm_spec), (fcw, _whole(fcw)), (big_f32[3], hbm_spec), (gf, _whole(gf))]
    yp, pp, cp, fp, ys, ps, cs, fs = pl.pallas_call(
        functools.partial(_layer_kernel, n_tiles),
        grid=(1 + steps,),
        in_specs=[spec for _, spec in operands],
        out_specs=[x_spec] + [
            pl.BlockSpec((None, r, w), lambda i: (tile_of(i)[0], 0, 0)) for r, w in zip(state_rows, state_widths)
        ] + [xs_spec, sp_spec, sc_spec, sf_spec],
        out_shape=[jax.ShapeDtypeStruct((batch, seq, D_MODEL), F32)] + [
            jax.ShapeDtypeStruct((batch, r, w), F32) for r, w in zip(state_rows, state_widths)
        ] + [jax.ShapeDtypeStruct(a.shape, F32) for a in (x_sample, sp, sc, sf)],
        scratch_shapes=[pltpu.VMEM(w.shape, BF16) for w in big_f32] + [
            pltpu.VMEM((STAGE_DEPTH, STAGE_ROWS, STAGE_COLS), F32),
            pltpu.SemaphoreType.DMA((STAGE_DEPTH,)),
            pltpu.VMEM((POOL_WIDTH, D_MODEL), F32),
            pltpu.SemaphoreType.DMA(()),
            pltpu.VMEM((POOL_PAD, POOL_WIDTH), F32),
            pltpu.VMEM((SUBLANES, CONV_WIDTH), F32),
            pltpu.VMEM((SUBLANES, 2 * D_FF), F32),
            pltpu.VMEM((ROWS, D_MODEL), BF16),
            pltpu.VMEM((ROWS, D_MODEL), BF16),
            pltpu.VMEM((POOL_WIDTH // LANES, POOL_PAD + TILE, LANES), F32),
            pltpu.VMEM((CONV_WIDTH // LANES, SUBLANES + TILE, LANES), F32),
            pltpu.VMEM((FF_BUFFERS, CHUNK_SLABS, SUBLANES + TILE, LANES), F32),
            pltpu.VMEM((POOL_WIDTH // LANES, RIDE_SEQ * POOL_PITCH, LANES), F32),
            pltpu.VMEM((CONV_WIDTH // LANES, RIDE_SEQ * CONV_PITCH, LANES), F32),
            pltpu.VMEM((FF_BUFFERS, CHUNK_SLABS, RIDE_SEQ * CONV_PITCH, LANES), F32),
        ] + [pltpu.VMEM((1, ROWS, FF_CHUNK), BF16)] * N_CHUNKS,
        compiler_params=pltpu.CompilerParams(
            dimension_semantics=("arbitrary",), vmem_limit_bytes=VMEM_LIMIT_BYTES),
        name="decoder_layer",
    )(*[a for a, _ in operands])

    return (yp, ys, pp[None], cp[None], fp[None], jnp.transpose(ps, (1, 0, 2))[None], cs[None], fs[None])
```

```python
import functools

import jax
import jax.numpy as jnp
import numpy as np
from jax import lax
from jax.experimental import pallas as pl
from jax.experimental.pallas import tpu as pltpu

D_MODEL = 1024
POOL_WIDTH = 512
CONV_WIDTH = 512
POOL_WINDOWS = (2, 4, 8, 16)
GROUP_DIM = 128
POOL_HIST = 15
CONV_K = 3
D_FF = 2816
IN_WIDTH = POOL_WIDTH + 3 * CONV_WIDTH
EPS = 1e-6

LANES = 128
SUBLANES = 8
TILE = 512
DEC_SEQ = 8
RIDE_SEQ = 4
RIDE_ROWS = RIDE_SEQ * DEC_SEQ
ROWS = TILE + RIDE_ROWS
FF_CHUNK = 256
N_CHUNKS = D_FF // FF_CHUNK
CHUNK_SLABS = 2 * FF_CHUNK // LANES
FF_BUFFERS = 3
POOL_PAD = 16
POOL_PITCH = POOL_PAD + DEC_SEQ
CONV_PITCH = SUBLANES + DEC_SEQ
STAGE_ROWS, STAGE_COLS = 256, 512
STAGE_DEPTH = 12
VMEM_LIMIT_BYTES = 58 * 1024 * 1024

F32 = jnp.float32
BF16 = jnp.bfloat16


def _rmsnorm(x, g):
    ms = jnp.sum(x * x, axis=-1, keepdims=True) * (1.0 / D_MODEL)
    return (x * lax.rsqrt(ms + EPS)) * g


def _silu(x):
    return x * (1.0 / (1.0 + jnp.exp(-x)))


def _lanes(i):
    return slice(i * LANES, (i + 1) * LANES)


def _ff_cols(c):
    g0 = c * FF_CHUNK
    v0 = D_FF + c * FF_CHUNK
    return slice(g0, g0 + FF_CHUNK), slice(v0, v0 + FF_CHUNK)


def _ff_slab_cols(c, s):
    gcols, vcols = _ff_cols(c)
    half = FF_CHUNK // LANES
    base = gcols.start if s < half else vcols.start
    off = (s % half) * LANES
    return slice(base + off, base + off + LANES)


def _taps(cw, s2, s1, s0):
    return cw[0] * s2 + cw[1] * s1 + cw[2] * s0


def _proj(h_ref, w_ref, cols):
    return jnp.dot(h_ref[...], w_ref[:, cols], preferred_element_type=F32)


def _conv_ffn(slot_ref, h_ref, act_refs, wup_ref, fcw_ref, wdown_ref, put_up, conv_up):
    def up_proj(c):
        gcols, vcols = _ff_cols(c)
        w_c = jnp.concatenate([wup_ref[:, gcols], wup_ref[:, vcols]], axis=1)
        return jnp.dot(h_ref[...], w_c, preferred_element_type=F32)

    put_up(0, up_proj(0))
    for c in range(N_CHUNKS):
        if c + 1 < N_CHUNKS:
            put_up(c + 1, up_proj(c + 1))
        gcols, vcols = _ff_cols(c)
        cw = jnp.concatenate([fcw_ref[:, :, gcols], fcw_ref[:, :, vcols]], axis=2)
        upc = conv_up(c, cw)
        act = _silu(upc[:, :FF_CHUNK]) * upc[:, FF_CHUNK:]
        act_refs[c][slot_ref[0]] = act.astype(BF16)

    down = None
    for c in range(N_CHUNKS):
        gcols, _ = _ff_cols(c)
        part = jnp.dot(act_refs[c][slot_ref[1]], wdown_ref[gcols, :], preferred_element_type=F32)
        down = part if down is None else down + part
    return down


def _cast_weights(srcs, dsts, stage, in_sems, after):
    depth = stage.shape[0]
    pieces = [(k, r, c) for k, src in enumerate(srcs)
              for r in range(0, src.shape[0], STAGE_ROWS) for c in range(0, src.shape[1], STAGE_COLS)]

    def fetch(p):
        k, r, c = pieces[p]
        window = srcs[k].at[r:r + STAGE_ROWS, c:c + STAGE_COLS]
        return pltpu.make_async_copy(window, stage.at[p % depth], in_sems.at[p % depth])

    for p in range(depth):
        fetch(p).start()
    for p, (k, r, c) in enumerate(pieces):
        fetch(p).wait()
        dsts[k][r:r + STAGE_ROWS, c:c + STAGE_COLS] = stage[p % depth].astype(BF16)
        if p + depth < len(pieces):
            fetch(p + depth).start()
        if k in after and (p + 1 == len(pieces) or pieces[p + 1][0] != k):
            after[k]()


def _layer_kernel(n_tiles, slot_ref, x_ref, xs_ref, sp_ref, sc_ref, sf_ref, g1_ref, win_hbm, wgrp_ref, scale_ref,
                  cw_ref, wout_hbm, g2_ref, wup_hbm, fcw_ref, wdown_hbm, gf_ref,
                  y_ref, po_ref, co_ref, fo_ref, ys_ref, pos_ref, cos_ref, fos_ref,
                  win_ref, wout_ref, wup_ref, wdown_ref, stage_ref, in_sems, wo32_ref, fold_sem,
                  hu_ref, hc_ref, hf_ref, h_ref, mix_ref, ubuf, cbuf, fbuf, usub, csub, fsub, *act_refs):
    i = pl.program_id(0)

    @pl.when(i == 0)
    def _():
        pool_rows = pltpu.make_async_copy(wout_hbm.at[0:POOL_WIDTH, :], wo32_ref, fold_sem)
        pool_rows.start()

        def fold_pool_map():
            pool_rows.wait()
            for g in range(len(POOL_WINDOWS)):
                folded = jnp.dot(wgrp_ref[g] * scale_ref[:, _lanes(g)], wo32_ref[_lanes(g), :],
                                 preferred_element_type=F32, precision=lax.Precision.HIGHEST)
                wout_ref[_lanes(g), :] = folded.astype(BF16)

        _cast_weights((win_hbm, wout_hbm, wup_hbm, wdown_hbm), (win_ref, wout_ref, wup_ref, wdown_ref),
                      stage_ref, in_sems, after={1: fold_pool_map})

    @pl.when(i > 0)
    def _():
        part = lax.rem(i - 1, SUBLANES // RIDE_SEQ)
        for p in range(SUBLANES // RIDE_SEQ):
            @pl.when(part == p)
            def _(p=p):
                for g in range(len(POOL_WINDOWS)):
                    for s in range(RIDE_SEQ):
                        usub[g, s * POOL_PITCH + 1:s * POOL_PITCH + POOL_PAD, :] = sp_ref[:, p * RIDE_SEQ + s, _lanes(g)]

        _tile(lax.rem(i - 1, n_tiles), slot_ref, x_ref, xs_ref, sc_ref, sf_ref, g1_ref, win_ref,
              cw_ref, wout_ref, g2_ref, wup_ref, fcw_ref, wdown_ref, gf_ref, y_ref, po_ref, co_ref, fo_ref,
              ys_ref, cos_ref, fos_ref, hu_ref, hc_ref, hf_ref, h_ref, mix_ref, ubuf, cbuf, fbuf, usub, csub, fsub,
              act_refs)

        for p in range(SUBLANES // RIDE_SEQ):
            @pl.when(part == p)
            def _(p=p):
                for g in range(len(POOL_WINDOWS)):
                    for s in range(RIDE_SEQ):
                        lo = s * POOL_PITCH + POOL_PITCH - POOL_HIST
                        pos_ref[:, p * RIDE_SEQ + s, _lanes(g)] = usub[g, lo:lo + POOL_HIST, :]


def _tile(j, slot_ref, x_ref, xs_ref, sc_ref, sf_ref, g1_ref, win_ref, cw_ref, wout_ref, g2_ref,
          wup_ref, fcw_ref, wdown_ref, gf_ref, y_ref, po_ref, co_ref, fo_ref, ys_ref, cos_ref, fos_ref,
          hu_ref, hc_ref, hf_ref, h_ref, mix_ref, ubuf, cbuf, fbuf, usub, csub, fsub, act_refs):
    seqs = range(RIDE_SEQ)

    def seq_rows(s):
        return slice(TILE + s * DEC_SEQ, TILE + (s + 1) * DEC_SEQ)

    @pl.when(j == 0)
    def _():
        hu_ref[...] = jnp.zeros_like(hu_ref)
        hc_ref[...] = jnp.zeros_like(hc_ref)
        hf_ref[...] = jnp.zeros_like(hf_ref)

    x = jnp.concatenate([x_ref[...], xs_ref[...].reshape(RIDE_ROWS, D_MODEL)], axis=0)
    h_ref[...] = _rmsnorm(x, g1_ref[...]).astype(BF16)

    u = _proj(h_ref, win_ref, slice(0, POOL_WIDTH))
    for g in range(len(POOL_WINDOWS)):
        ubuf[g, 0:POOL_PAD, :] = hu_ref[:, _lanes(g)]
        ubuf[g, POOL_PAD:, :] = u[:TILE, _lanes(g)]
        hu_ref[:, _lanes(g)] = u[TILE - POOL_PAD:TILE, _lanes(g)]
        for s in seqs:
            usub[g, s * POOL_PITCH + POOL_PAD:(s + 1) * POOL_PITCH, :] = u[seq_rows(s), _lanes(g)]
    po_ref[...] = hu_ref[POOL_PAD - POOL_HIST:, :]

    gc = _proj(h_ref, win_ref, slice(POOL_WIDTH + CONV_WIDTH, POOL_WIDTH + 2 * CONV_WIDTH))
    hv = _proj(h_ref, win_ref, slice(POOL_WIDTH + 2 * CONV_WIDTH, IN_WIDTH))
    v = gc * hv
    for sl in range(CONV_WIDTH // LANES):
        cbuf[sl, 0:SUBLANES, :] = hc_ref[:, _lanes(sl)]
        cbuf[sl, SUBLANES:, :] = v[:TILE, _lanes(sl)]
        hc_ref[:, _lanes(sl)] = v[TILE - SUBLANES:TILE, _lanes(sl)]
        for s in seqs:
            base = s * CONV_PITCH + SUBLANES
            csub[sl, base - (CONV_K - 1):base, :] = sc_ref[s, :, _lanes(sl)]
            csub[sl, base:base + DEC_SEQ, :] = v[seq_rows(s), _lanes(sl)]
            cos_ref[s, :, _lanes(sl)] = v[TILE + (s + 1) * DEC_SEQ - (CONV_K - 1):TILE + (s + 1) * DEC_SEQ, _lanes(sl)]
    co_ref[...] = hc_ref[SUBLANES - (CONV_K - 1):, :]
    gb = _proj(h_ref, win_ref, slice(POOL_WIDTH, POOL_WIDTH + CONV_WIDTH))

    pos = lax.broadcasted_iota(jnp.int32, (TILE, GROUP_DIM), 0) + j * TILE
    for g, win in enumerate(POOL_WINDOWS):
        cur = ubuf[g, POOL_PAD:, :]
        s_p = cur
        for k in range(1, win):
            s_p = s_p + ubuf[g, POOL_PAD - k:POOL_PAD - k + TILE, :]
        cnt = jnp.minimum(pos + 1, win).astype(F32)
        d = [s_p / cnt - cur]
        for s in seqs:
            base = s * POOL_PITCH + POOL_PAD
            cur_s = usub[g, base:base + DEC_SEQ, :]
            s_s = cur_s
            for k in range(1, win):
                s_s = s_s + usub[g, base - k:base - k + DEC_SEQ, :]
            d.append(s_s * (1.0 / win) - cur_s)
        mix_ref[:, _lanes(g)] = jnp.concatenate(d, axis=0).astype(BF16)

    for sl in range(CONV_WIDTH // LANES):
        taps = [[cbuf[sl, SUBLANES - k:SUBLANES - k + TILE, :]] for k in (2, 1, 0)]
        for s in seqs:
            base = s * CONV_PITCH + SUBLANES
            for t, k in zip(taps, (2, 1, 0)):
                t.append(csub[sl, base - k:base - k + DEC_SEQ, :])
        yc = gb[:, _lanes(sl)] * _taps(cw_ref[:, :, _lanes(sl)], *[jnp.concatenate(t, axis=0) for t in taps])
        mix_ref[:, POOL_WIDTH + sl * LANES:POOL_WIDTH + (sl + 1) * LANES] = yc.astype(BF16)

    x1 = x + jnp.dot(mix_ref[...], wout_ref[...], preferred_element_type=F32)
    h_ref[...] = _rmsnorm(x1, g2_ref[...]).astype(BF16)

    def put_up(c, up):
        buf, sub = fbuf.at[c % FF_BUFFERS], fsub.at[c % FF_BUFFERS]
        for sl in range(CHUNK_SLABS):
            cols = _ff_slab_cols(c, sl)
            buf[sl, 0:SUBLANES, :] = hf_ref[:, cols]
            buf[sl, SUBLANES:, :] = up[:TILE, _lanes(sl)]
            hf_ref[:, cols] = up[TILE - SUBLANES:TILE, _lanes(sl)]
            for s in seqs:
                base = s * CONV_PITCH + SUBLANES
                sub[sl, base - (CONV_K - 1):base, :] = sf_ref[s, :, cols]
                sub[sl, base:base + DEC_SEQ, :] = up[seq_rows(s), _lanes(sl)]
                fos_ref[s, :, cols] = up[TILE + (s + 1) * DEC_SEQ - (CONV_K - 1):TILE + (s + 1) * DEC_SEQ, _lanes(sl)]

    def conv_up(c, cw):
        buf, sub = fbuf.at[c % FF_BUFFERS], fsub.at[c % FF_BUFFERS]
        outs = []
        for sl in range(CHUNK_SLABS):
            taps = [[buf[sl, SUBLANES - k:SUBLANES - k + TILE, :]] for k in (2, 1, 0)]
            for s in seqs:
                base = s * CONV_PITCH + SUBLANES
                for t, k in zip(taps, (2, 1, 0)):
                    t.append(sub[sl, base - k:base - k + DEC_SEQ, :])
            outs.append(_taps(cw[:, :, _lanes(sl)], *[jnp.concatenate(t, axis=0) for t in taps]))
        return jnp.concatenate(outs, axis=1)

    x2 = x1 + _conv_ffn(slot_ref, h_ref, act_refs, wup_ref, fcw_ref, wdown_ref, put_up, conv_up)
    fo_ref[...] = hf_ref[SUBLANES - (CONV_K - 1):, :]
    y = _rmsnorm(x2, gf_ref[...])
    y_ref[...] = y[:TILE]
    ys_ref[...] = y[TILE:].reshape(RIDE_SEQ, DEC_SEQ, D_MODEL)


def _whole(a):
    nd = a.ndim
    return pl.BlockSpec(a.shape, lambda *_: (0,) * nd, pipeline_mode=pl.Buffered(1))


def kernel(x_prompt, x_sample, state_pool, state_conv, state_ffn, norm1_g, w_in, w_pool_grp, pool_scale,
           conv_w, w_out, norm2_g, w_up, ffn_conv_w, w_down, final_g):
    depth = w_in.shape[0]
    assert depth == 1
    batch, seq, _ = x_prompt.shape
    dec_batch, dec_seq, _ = x_sample.shape
    n_tiles = seq // TILE
    steps = batch * n_tiles
    assert seq % TILE == 0 and dec_seq == DEC_SEQ and dec_batch == RIDE_SEQ * steps and SUBLANES % RIDE_SEQ == 0
    assert all(w.shape[1] % STAGE_ROWS == 0 and w.shape[2] % STAGE_COLS == 0 for w in (w_in, w_out, w_up, w_down))

    big_f32 = (w_in[0], w_out[0], w_up[0], w_down[0])
    g1, g2, gf = norm1_g[0].reshape(1, D_MODEL), norm2_g[0].reshape(1, D_MODEL), final_g.reshape(1, D_MODEL)
    wgrp, scale = w_pool_grp[0], pool_scale[0].reshape(1, POOL_WIDTH)
    cw, fcw = jnp.transpose(conv_w, (1, 0, 2)), jnp.transpose(ffn_conv_w, (1, 0, 2))
    sp = jnp.transpose(state_pool[0], (1, 0, 2))
    sc, sf = state_conv[0], state_ffn[0]
    slots = np.zeros((2,), np.int32)
    slot_spec = pl.BlockSpec(memory_space=pltpu.SMEM)
    hbm_spec = pl.BlockSpec(memory_space=pl.ANY)

    def tile_of(i):
        t = jnp.maximum(i - 1, 0)
        return t // n_tiles, t % n_tiles

    def ride_of(i):
        return jnp.maximum(i - 1, 0)

    x_spec = pl.BlockSpec((None, TILE, D_MODEL), lambda i: (*tile_of(i), 0))
    xs_spec = pl.BlockSpec((RIDE_SEQ, DEC_SEQ, D_MODEL), lambda i: (ride_of(i), 0, 0))
    sp_spec = pl.BlockSpec((POOL_HIST, SUBLANES, POOL_WIDTH), lambda i: (0, ride_of(i) // (SUBLANES // RIDE_SEQ), 0))
    sc_spec = pl.BlockSpec((RIDE_SEQ, CONV_K - 1, CONV_WIDTH), lambda i: (ride_of(i), 0, 0))
    sf_spec = pl.BlockSpec((RIDE_SEQ, CONV_K - 1, 2 * D_FF), lambda i: (ride_of(i), 0, 0))
    state_rows = (POOL_HIST, CONV_K - 1, CONV_K - 1)
    state_widths = (POOL_WIDTH, CONV_WIDTH, 2 * D_FF)

    operands = [(slots, slot_spec), (x_prompt, x_spec), (x_sample, xs_spec), (sp, sp_spec), (sc, sc_spec),
                (sf, sf_spec), (g1, _whole(g1)), (big_f32[0], hbm_spec), (wgrp, _whole(wgrp)),
                (scale, _whole(scale)), (cw, _whole(cw)), (big_f32[1], hbm_spec), (g2, _whole(g2)),
                (big_f32[2], hbm_spec), (fcw, _whole(fcw)), (big_f32[3], hbm_spec), (gf, _whole(gf))]
    yp, pp, cp, fp, ys, ps, cs, fs = pl.pallas_call(
        functools.partial(_layer_kernel, n_tiles),
        grid=(1 + steps,),
        in_specs=[spec for _, spec in operands],
        out_specs=[x_spec] + [
            pl.BlockSpec((None, r, w), lambda i: (tile_of(i)[0], 0, 0)) for r, w in zip(state_rows, state_widths)
        ] + [xs_spec, sp_spec, sc_spec, sf_spec],
        out_shape=[jax.ShapeDtypeStruct((batch, seq, D_MODEL), F32)] + [
            jax.ShapeDtypeStruct((batch, r, w), F32) for r, w in zip(state_rows, state_widths)
        ] + [jax.ShapeDtypeStruct(a.shape, F32) for a in (x_sample, sp, sc, sf)],
        scratch_shapes=[pltpu.VMEM(w.shape, BF16) for w in big_f32] + [
            pltpu.VMEM((STAGE_DEPTH, STAGE_ROWS, STAGE_COLS), F32),
            pltpu.SemaphoreType.DMA((STAGE_DEPTH,)),
            pltpu.VMEM((POOL_WIDTH, D_MODEL), F32),
            pltpu.SemaphoreType.DMA(()),
            pltpu.VMEM((POOL_PAD, POOL_WIDTH), F32),
            pltpu.VMEM((SUBLANES, CONV_WIDTH), F32),
            pltpu.VMEM((SUBLANES, 2 * D_FF), F32),
            pltpu.VMEM((ROWS, D_MODEL), BF16),
            pltpu.VMEM((ROWS, D_MODEL), BF16),
            pltpu.VMEM((POOL_WIDTH // LANES, POOL_PAD + TILE, LANES), F32),
            pltpu.VMEM((CONV_WIDTH // LANES, SUBLANES + TILE, LANES), F32),
            pltpu.VMEM((FF_BUFFERS, CHUNK_SLABS, SUBLANES + TILE, LANES), F32),
            pltpu.VMEM((POOL_WIDTH // LANES, RIDE_SEQ * POOL_PITCH, LANES), F32),
            pltpu.VMEM((CONV_WIDTH // LANES, RIDE_SEQ * CONV_PITCH, LANES), F32),
            pltpu.VMEM((FF_BUFFERS, CHUNK_SLABS, RIDE_SEQ * CONV_PITCH, LANES), F32),
        ] + [pltpu.VMEM((1, ROWS, FF_CHUNK), BF16)] * N_CHUNKS,
        compiler_params=pltpu.CompilerParams(
            dimension_semantics=("arbitrary",), vmem_limit_bytes=VMEM_LIMIT_BYTES),
        name="decoder_layer",
    )(*[a for a, _ in operands])

    return (yp, ys, pp[None], cp[None], fp[None], jnp.transpose(ps, (1, 0, 2))[None], cs[None], fs[None])
```
